```python
import math
import jax, jax.numpy as jnp
from jax import lax
import numpy as np

D_MODEL = 4096
BATCH = 4
SEQ = 2048
DEPTH = 1
DEC_BATCH = 2
DEC_SEQ = 4096
PAST_LEN = 128

N_MEM = 256
XA_HEADS = 4
XA_HEAD_DIM = D_MODEL // XA_HEADS
MLA_HEADS = 16
MLA_NOPE = 128
MLA_ROPE = 64
MLA_V = 128
Q_LORA = 1024
KV_LORA = 512
ROPE_THETA = 10000.0
Q_BLOCK = 128
MLA_WIDTH = MLA_HEADS * MLA_V
GLA_HEADS = 4
GLA_DK = 256
GLA_DV = 512
GLA_KEY = GLA_HEADS * GLA_DK
GLA_VAL = GLA_HEADS * GLA_DV
GLA_GATE_RANK = 16
GLA_GATE_NORM = 16.0
GLA_CHUNK = 64
D_FF = 4 * D_MODEL
LN_EPS = 1e-5
RMS_EPS = 1e-6
DN_ALPHA = (2.0 * DEPTH) ** 0.25
DN_BETA = (8.0 * DEPTH) ** -0.25
IN_SIZES = (Q_LORA, KV_LORA, MLA_ROPE, GLA_KEY, GLA_KEY, GLA_VAL, GLA_VAL, 2 * GLA_GATE_RANK, 2 * D_MODEL)
IN_VALUE_BLOCK = 5
IN_TOTAL = sum(IN_SIZES)

kernel_name = "hybrid_mla_gla_deepnorm_encoder"


def _layer_norm(x, g, b):
    xf = x.astype(jnp.float32)
    mu = jnp.mean(xf, -1, keepdims=True)
    var = jnp.mean(jnp.square(xf - mu), -1, keepdims=True)
    y = (xf - mu) * lax.rsqrt(var + LN_EPS)
    return (y * g.astype(jnp.float32) + b.astype(jnp.float32)).astype(x.dtype)


def _rms_norm(x, g):
    xf = x.astype(jnp.float32)
    y = xf * lax.rsqrt(jnp.mean(jnp.square(xf), -1, keepdims=True) + RMS_EPS)
    return (y * g.astype(jnp.float32)).astype(x.dtype)


def _rope_tables(seq_len):
    inv = 1.0 / (ROPE_THETA ** (jnp.arange(0, MLA_ROPE, 2, dtype=jnp.float32) / MLA_ROPE))
    ang = jnp.arange(seq_len, dtype=jnp.float32)[:, None] * inv[None, :]
    return jnp.cos(ang)[:, None, :], jnp.sin(ang)[:, None, :]


def _apply_rope(x, cos, sin):
    xf = x.astype(jnp.float32)
    x1, x2 = jnp.split(xf, 2, axis=-1)
    return jnp.concatenate([x1 * cos - x2 * sin, x2 * cos + x1 * sin], axis=-1).astype(x.dtype)


def _mla(c_q, c_kv, k_rope, q_norm, w_uq, kv_norm, w_ukv, cos, sin):
    b, s, _ = c_q.shape
    q = (_rms_norm(c_q, q_norm) @ w_uq).reshape(b, s, MLA_HEADS, MLA_NOPE + MLA_ROPE)
    q_nope = q[..., :MLA_NOPE]
    q_rope = _apply_rope(q[..., MLA_NOPE:], cos, sin)
    kv = (_rms_norm(c_kv, kv_norm) @ w_ukv).reshape(b, s, MLA_HEADS, MLA_NOPE + MLA_V)
    k_nope, v = kv[..., :MLA_NOPE], kv[..., MLA_NOPE:]
    k_r = _apply_rope(k_rope[:, :, None, :], cos, sin)[:, :, 0]
    scale = (MLA_NOPE + MLA_ROPE) ** -0.5
    n_blk = s // Q_BLOCK

    def block(qs):
        qn, qr = qs
        sc = (jnp.einsum('bqhd,bkhd->bhqk', qn, k_nope)
              + jnp.einsum('bqhr,bkr->bhqk', qr, k_r))
        p = jax.nn.softmax(sc.astype(jnp.float32) * scale, axis=-1).astype(v.dtype)
        return jnp.einsum('bhqk,bkhd->bqhd', p, v)

    to_blocks = lambda t: jnp.moveaxis(t.reshape(b, n_blk, Q_BLOCK, *t.shape[2:]), 1, 0)
    o = lax.map(block, (to_blocks(q_nope), to_blocks(q_rope)))
    return jnp.moveaxis(o, 0, 1).reshape(b, s, MLA_WIDTH)


def _gla_direction(q, k, v, log_a):
    b, h, s, dk = q.shape
    dv = v.shape[-1]
    n = s // GLA_CHUNK
    rs = lambda t: t.reshape(b, h, n, GLA_CHUNK, t.shape[-1])
    q, k, v, log_a = rs(q), rs(k), rs(v), rs(log_a)
    cum = jnp.cumsum(log_a, axis=3)
    last = cum[:, :, :, -1:, :]
    q_dec = q * jnp.exp(cum)
    k_intra = k * jnp.exp(-cum)
    k_to_end = k * jnp.exp(last - cum)
    mask = jnp.tril(jnp.ones((GLA_CHUNK, GLA_CHUNK), dtype=bool))
    attn = jnp.where(mask, jnp.einsum('bhncd,bhnjd->bhncj', q_dec, k_intra), 0.0)
    o_intra = jnp.einsum('bhncj,bhnjv->bhncv', attn, v)
    chunk_decay = jnp.exp(last[:, :, :, 0, :])

    def step(state, inp):
        q_c, k_c, v_c, dec_c = inp
        out = jnp.einsum('bhcd,bhdv->bhcv', q_c, state)
        new = dec_c[..., None] * state + jnp.einsum('bhcd,bhcv->bhdv', k_c, v_c)
        return new, out

    init = jnp.zeros((b, h, dk, dv), jnp.float32)
    mv = lambda t: jnp.moveaxis(t, 2, 0)
    _, o_inter = lax.scan(step, init, (mv(q_dec), mv(k_to_end), mv(v), mv(chunk_decay)))
    o = o_intra + jnp.moveaxis(o_inter, 0, 2)
    return o.reshape(b, h, s, dv)


def _gla(gq, gk, gv, gr, g_lr, w2, gb, norm_g):
    b, s, _ = gq.shape
    heads = lambda t, d: jnp.moveaxis(t.reshape(b, s, GLA_HEADS, d), 2, 1).astype(jnp.float32)
    q = heads(gq, GLA_DK) * (GLA_DK ** -0.5)
    k = heads(gk, GLA_DK)
    v = heads(gv, GLA_DV)
    lr = g_lr.reshape(b, s, 2, GLA_GATE_RANK)
    gate_pre = jnp.einsum('bstr,trk->tbsk', lr, w2) + gb[:, None, None, :]
    log_a = jax.nn.log_sigmoid(gate_pre.astype(jnp.float32)) / GLA_GATE_NORM
    flip = lambda t: jnp.flip(t, axis=2)
    fwd = _gla_direction(q, k, v, heads(log_a[0], GLA_DK))
    bwd = flip(_gla_direction(flip(q), flip(k), flip(v), flip(heads(log_a[1], GLA_DK))))
    o = _rms_norm(jnp.moveaxis(fwd + bwd, 1, 2), norm_g)
    o = o.reshape(b, s, GLA_VAL) * jax.nn.silu(gr.astype(jnp.float32))
    return o.astype(gq.dtype)


def _mixer(h, w_in, b_merge, q_norm, w_uq, kv_norm, w_ukv, gate_w2, gate_b, gla_norm,
           w_br_mla, w_br_gla, w_out, cos, sin):
    b, s, _ = h.shape
    offs = np.cumsum(IN_SIZES)[:-1].tolist()
    c_q, c_kv, k_rope, gq, gk, gv, gr, g_lr, g_merge = jnp.split(h @ w_in, offs, axis=-1)
    o_mla = _mla(c_q, c_kv, k_rope, q_norm, w_uq, kv_norm, w_ukv, cos, sin)
    o_gla = _gla(gq, gk, gv, gr, g_lr, gate_w2, gate_b, gla_norm)
    gates = jax.nn.sigmoid((g_merge + b_merge).astype(jnp.float32)).astype(h.dtype).reshape(b, s, 2, D_MODEL)
    merged = gates[:, :, 0] * (o_mla @ w_br_mla) + gates[:, :, 1] * (o_gla @ w_br_gla)
    return merged @ w_out


def _cross_attn(h, mem, wq, wkv, wo):
    b, s, _ = h.shape
    q = (h @ wq).reshape(b, s, XA_HEADS, XA_HEAD_DIM)
    kv = (mem @ wkv).reshape(b, mem.shape[1], 2, XA_HEADS, XA_HEAD_DIM)
    k, v = kv[:, :, 0], kv[:, :, 1]
    sc = jnp.einsum('bqhd,bmhd->bhqm', q, k).astype(jnp.float32) * (XA_HEAD_DIM ** -0.5)
    p = jax.nn.softmax(sc, axis=-1).astype(h.dtype)
    o = jnp.einsum('bhqm,bmhd->bqhd', p, v).reshape(b, s, D_MODEL)
    return o @ wo


def _encode(x, mem, ln_in_g, ln_in_b, w_in, b_merge, mla_q_norm, w_uq, mla_kv_norm, w_ukv,
            gla_gate_w2, gla_gate_b, gla_norm, w_branch_mla, w_branch_gla, w_mix_out,
            ln1_g, ln1_b, xa_wq, xa_wkv, xa_wo, ln2_g, ln2_b, mlp_w1, mlp_w2, ln3_g, ln3_b):
    cos, sin = _rope_tables(x.shape[1])
    h = _layer_norm(x, ln_in_g, ln_in_b)
    for l in range(DEPTH):
        mix = _mixer(h, w_in[l], b_merge[l], mla_q_norm[l], w_uq[l], mla_kv_norm[l], w_ukv[l],
                     gla_gate_w2[l], gla_gate_b[l], gla_norm[l], w_branch_mla[l], w_branch_gla[l],
                     w_mix_out[l], cos, sin)
        h = _layer_norm(DN_ALPHA * h + mix, ln1_g[l], ln1_b[l])
        h = _layer_norm(DN_ALPHA * h + _cross_attn(h, mem, xa_wq[l], xa_wkv[l], xa_wo[l]), ln2_g[l], ln2_b[l])
        ff = jnp.square(jax.nn.relu(h @ mlp_w1[l])) @ mlp_w2[l]
        h = _layer_norm(DN_ALPHA * h + ff, ln3_g[l], ln3_b[l])
    return h


def _w(k, shape, fan_in, scale=1.0):
    return jax.random.normal(k, shape, jnp.float32) * (scale * fan_in ** -0.5)


def setup_inputs(seed: int = 0) -> dict:
    key = jax.random.key(seed)
    ks = iter(jax.random.split(key, 64))
    gain = lambda shape: 1.0 + 0.02 * jax.random.normal(next(ks), shape, jnp.float32)
    bias = lambda shape: 0.02 * jax.random.normal(next(ks), shape, jnp.float32)
    L = DEPTH
    x_prompt = jax.random.normal(next(ks), (BATCH, SEQ, D_MODEL), jnp.float32)
    x_sample = jax.random.normal(next(ks), (DEC_BATCH, DEC_SEQ, D_MODEL), jnp.float32)
    mem_prompt = jax.random.normal(next(ks), (BATCH, N_MEM, D_MODEL), jnp.float32)
    mem_sample = jax.random.normal(next(ks), (DEC_BATCH, N_MEM, D_MODEL), jnp.float32)
    ln_in_g = gain((D_MODEL,))
    ln_in_b = bias((D_MODEL,))
    w_in = jnp.concatenate(
        [_w(next(ks), (L, D_MODEL, n), D_MODEL, DN_BETA if i == IN_VALUE_BLOCK else 1.0)
         for i, n in enumerate(IN_SIZES)], axis=-1)
    b_merge = bias((L, 2 * D_MODEL))
    mla_q_norm = gain((L, Q_LORA))
    w_uq = _w(next(ks), (L, Q_LORA, MLA_HEADS * (MLA_NOPE + MLA_ROPE)), Q_LORA)
    mla_kv_norm = gain((L, KV_LORA))
    w_uk = _w(next(ks), (L, KV_LORA, MLA_HEADS, MLA_NOPE), KV_LORA)
    w_uv = _w(next(ks), (L, KV_LORA, MLA_HEADS, MLA_V), KV_LORA, DN_BETA)
    w_ukv = jnp.concatenate([w_uk, w_uv], axis=-1).reshape(L, KV_LORA, MLA_HEADS * (MLA_NOPE + MLA_V))
    gla_gate_w2 = _w(next(ks), (L, 2, GLA_GATE_RANK, GLA_KEY), GLA_GATE_RANK)
    gla_gate_b = bias((L, 2, GLA_KEY))
    gla_norm = gain((L, GLA_DV))
    w_branch_mla = _w(next(ks), (L, MLA_WIDTH, D_MODEL), MLA_WIDTH)
    w_branch_gla = _w(next(ks), (L, GLA_VAL, D_MODEL), GLA_VAL)
    w_mix_out = _w(next(ks), (L, D_MODEL, D_MODEL), D_MODEL, DN_BETA)
    ln1_g = gain((L, D_MODEL))
    ln1_b = bias((L, D_MODEL))
    xa_wq = _w(next(ks), (L, D_MODEL, D_MODEL), D_MODEL)
    xa_wkv = jnp.concatenate([_w(next(ks), (L, D_MODEL, D_MODEL), D_MODEL),
                              _w(next(ks), (L, D_MODEL, D_MODEL), D_MODEL, DN_BETA)], axis=-1)
    xa_wo = _w(next(ks), (L, D_MODEL, D_MODEL), D_MODEL, DN_BETA)
    ln2_g = gain((L, D_MODEL))
    ln2_b = bias((L, D_MODEL))
    mlp_w1 = _w(next(ks), (L, D_MODEL, D_FF), D_MODEL)
    mlp_w2 = _w(next(ks), (L, D_FF, D_MODEL), D_FF, DN_BETA)
    ln3_g = gain((L, D_MODEL))
    ln3_b = bias((L, D_MODEL))
    return {"x_prompt": x_prompt, "x_sample": x_sample, "mem_prompt": mem_prompt, "mem_sample": mem_sample,
            "ln_in_g": ln_in_g, "ln_in_b": ln_in_b, "w_in": w_in, "b_merge": b_merge,
            "mla_q_norm": mla_q_norm, "w_uq": w_uq, "mla_kv_norm": mla_kv_norm, "w_ukv": w_ukv,
            "gla_gate_w2": gla_gate_w2, "gla_gate_b": gla_gate_b, "gla_norm": gla_norm,
            "w_branch_mla": w_branch_mla, "w_branch_gla": w_branch_gla, "w_mix_out": w_mix_out,
            "ln1_g": ln1_g, "ln1_b": ln1_b, "xa_wq": xa_wq, "xa_wkv": xa_wkv, "xa_wo": xa_wo,
            "ln2_g": ln2_g, "ln2_b": ln2_b, "mlp_w1": mlp_w1, "mlp_w2": mlp_w2,
            "ln3_g": ln3_g, "ln3_b": ln3_b}


def reference(x_prompt, x_sample, mem_prompt, mem_sample, ln_in_g, ln_in_b, w_in, b_merge,
              mla_q_norm, w_uq, mla_kv_norm, w_ukv, gla_gate_w2, gla_gate_b, gla_norm,
              w_branch_mla, w_branch_gla, w_mix_out, ln1_g, ln1_b, xa_wq, xa_wkv, xa_wo,
              ln2_g, ln2_b, mlp_w1, mlp_w2, ln3_g, ln3_b):
    params = (ln_in_g, ln_in_b, w_in, b_merge, mla_q_norm, w_uq, mla_kv_norm, w_ukv,
              gla_gate_w2, gla_gate_b, gla_norm, w_branch_mla, w_branch_gla, w_mix_out,
              ln1_g, ln1_b, xa_wq, xa_wkv, xa_wo, ln2_g, ln2_b, mlp_w1, mlp_w2, ln3_g, ln3_b)
    y_prompt = _encode(x_prompt, mem_prompt, *params)
    y_sample = _encode(x_sample, mem_sample, *params)
    return (y_prompt, y_sample)
```

```python
import functools

import numpy as np
import jax
import jax.numpy as jnp
from jax import lax
from jax.experimental import pallas as pl
from jax.experimental.pallas import tpu as pltpu

MLA_HEADS = 16
MLA_NOPE = 128
MLA_ROPE = 64
MLA_V = 128
ROPE_THETA = 10000.0
GLA_HEADS = 4
GLA_DK = 256
GLA_DV = 512
GLA_GATE_RANK = 16
GLA_GATE_NORM = 16.0
GLA_CHUNK = 64
XA_HEADS = 4
LN_EPS = 1e-5
RMS_EPS = 1e-6
DEPTH = 1
DN_ALPHA = (2.0 * DEPTH) ** 0.25

LANES = 128
MLA_QK_PAD = 2 * LANES
VMEM_LIMIT_BYTES = 56 * 2**20

BF16 = jnp.bfloat16
F32 = jnp.float32
NT_DIMS = (((1,), (1,)), ((), ()))
TN_DIMS = (((0,), (0,)), ((), ()))


def _params(n_grid):
    return pltpu.CompilerParams(dimension_semantics=("arbitrary",) * n_grid,
                                vmem_limit_bytes=VMEM_LIMIT_BYTES)


def _block(n, pref):
    b = min(n, pref)
    while n % b:
        b //= 2
    return b


def _round_up(n, m):
    return -(-n // m) * m


def _layer_norm(x, g, b):
    mu = jnp.mean(x, -1, keepdims=True)
    xc = x - mu
    var = jnp.mean(xc * xc, -1, keepdims=True)
    return xc * lax.rsqrt(var + LN_EPS) * g + b


def _rms_norm(x, g):
    return x * lax.rsqrt(jnp.mean(x * x, -1, keepdims=True) + RMS_EPS) * g


def _sigmoid(x):
    return 1.0 / (1.0 + jnp.exp(-x))


def _rope(x, cos, sin):
    return x * cos + pltpu.roll(x, MLA_ROPE, 1) * sin


def _ln_in_kernel(xa_ref, xb_ref, g_ref, b_ref, h_ref, hb_ref, *, n_a):
    def emit(x_ref):
        y = _layer_norm(x_ref[...], g_ref[...], b_ref[...])
        h_ref[...] = y
        hb_ref[...] = y.astype(BF16)

    @pl.when(pl.program_id(0) < n_a)
    def _():
        emit(xa_ref)

    @pl.when(pl.program_id(0) >= n_a)
    def _():
        emit(xb_ref)


def _ln_in(xa, xb, g, b):
    ta, d = xa.shape
    tb = xb.shape[0]
    bm = _block(np.gcd(ta, tb), 256)
    n_a, n_b = ta // bm, tb // bm
    row = pl.BlockSpec((1, d), lambda i: (0, 0))
    out = pl.BlockSpec((bm, d), lambda i: (i, 0))
    return pl.pallas_call(
        functools.partial(_ln_in_kernel, n_a=n_a),
        grid=(n_a + n_b,),
        in_specs=[pl.BlockSpec((bm, d), lambda i: (jnp.minimum(i, n_a - 1), 0)),
                  pl.BlockSpec((bm, d), lambda i: (jnp.maximum(i - n_a, 0), 0)),
                  row, row],
        out_specs=[out, out],
        out_shape=[jax.ShapeDtypeStruct((ta + tb, d), F32),
                   jax.ShapeDtypeStruct((ta + tb, d), BF16)],
        compiler_params=_params(1),
        name="ln_in",
    )(xa, xb, g.reshape(1, d), b.reshape(1, d))


def _res_ln_kernel(h_ref, y_ref, g_ref, b_ref, *o_refs):
    z = _layer_norm(DN_ALPHA * h_ref[...] + y_ref[...], g_ref[...], b_ref[...])
    for o_ref in o_refs:
        o_ref[...] = z.astype(o_ref.dtype)


def _res_ln(h, y, g, b, out_dtypes, *, row_off=0, rows=None):
    t, d = h.shape
    rows = t if rows is None else rows
    bm = _block(np.gcd(rows, row_off) if row_off else rows, 256)
    off = row_off // bm
    row = pl.BlockSpec((1, d), lambda i: (0, 0))
    src = pl.BlockSpec((bm, d), lambda i: (i + off, 0))
    dst = pl.BlockSpec((bm, d), lambda i: (i, 0))
    return pl.pallas_call(
        _res_ln_kernel,
        grid=(rows // bm,),
        in_specs=[src, src, row, row],
        out_specs=[dst] * len(out_dtypes),
        out_shape=[jax.ShapeDtypeStruct((rows, d), dt) for dt in out_dtypes],
        compiler_params=_params(1),
        name="res_ln",
    )(h, y, g.reshape(1, d), b.reshape(1, d))


def _matmul_kernel(a_ref, b_ref, o_ref, *scratch, nk, relu2):
    part = jnp.dot(a_ref[...], b_ref[...], preferred_element_type=F32)

    def finish(acc):
        if relu2:
            acc = jnp.square(jnp.maximum(acc, 0.0))
        o_ref[...] = acc.astype(o_ref.dtype)

    if nk == 1:
        finish(part)
        return
    acc_ref, = scratch
    k = pl.program_id(2)

    @pl.when(k == 0)
    def _():
        acc_ref[...] = part

    @pl.when(jnp.logical_and(k > 0, k < nk - 1))
    def _():
        acc_ref[...] += part

    @pl.when(k == nk - 1)
    def _():
        finish(acc_ref[...] + part)


def _matmul(a, b, out_dtype, *, bm=1024, bn=1024, bk=4096, relu2=False, name="matmul"):
    m, kdim = a.shape
    n = b.shape[1]
    bm, bn, bk = _block(m, bm), _block(n, bn), _block(kdim, bk)
    nk = kdim // bk
    return pl.pallas_call(
        functools.partial(_matmul_kernel, nk=nk, relu2=relu2),
        grid=(m // bm, n // bn, nk),
        in_specs=[pl.BlockSpec((bm, bk), lambda i, j, k: (i, k)),
                  pl.BlockSpec((bk, bn), lambda i, j, k: (k, j))],
        out_specs=pl.BlockSpec((bm, bn), lambda i, j, k: (i, j)),
        out_shape=jax.ShapeDtypeStruct((m, n), out_dtype),
        scratch_shapes=[pltpu.VMEM((bm, bn), F32)] if nk > 1 else [],
        compiler_params=_params(3),
        name=name,
    )(a, b)


def _rope_tables(seq_len):
    half = MLA_ROPE // 2
    inv = 1.0 / (ROPE_THETA ** (jnp.arange(0, MLA_ROPE, 2, dtype=F32) / MLA_ROPE))
    ang = jnp.arange(seq_len, dtype=F32)[:, None] * inv[None, :]
    zero = jnp.zeros((seq_len, LANES - 2 * half), F32)
    cos = jnp.concatenate([jnp.cos(ang), jnp.cos(ang), zero], axis=1)
    sin = jnp.concatenate([jnp.sin(ang), jnp.sin(ang), zero], axis=1)
    return cos, sin


def _pos_block_map(groups, bm):
    (_, _, s_a), (off_b, _, s_b) = groups
    n_a = off_b // bm

    def index(i):
        return jnp.where(i < n_a, i % (s_a // bm), (i - n_a) % (s_b // bm))
    return index


def _qproj_kernel(c_ref, g_ref, w_ref, cos_ref, sin_ref, o_ref, xn_ref, *, heads, scale):
    @pl.when(pl.program_id(1) == 0)
    def _():
        xn_ref[...] = _rms_norm(c_ref[...], g_ref[...]).astype(BF16)

    r = jnp.dot(xn_ref[...], w_ref[...], preferred_element_type=F32)
    cos, sin = cos_ref[...], sin_ref[...]
    for h in range(heads):
        lo = h * MLA_QK_PAD
        o_ref[:, lo:lo + MLA_NOPE] = (r[:, lo:lo + MLA_NOPE] * scale).astype(BF16)
        o_ref[:, lo + MLA_NOPE:lo + MLA_QK_PAD] = (
            _rope(r[:, lo + MLA_NOPE:lo + MLA_QK_PAD], cos, sin) * scale).astype(BF16)


def _q_proj(p, c_off, q_lora, g, wq, cos, sin, groups, bm):
    t = p.shape[0]
    heads = min(4, MLA_HEADS)
    bn = heads * MLA_QK_PAD
    pos = _pos_block_map(groups, bm)
    scale = (MLA_NOPE + MLA_ROPE) ** -0.5
    return pl.pallas_call(
        functools.partial(_qproj_kernel, heads=heads, scale=scale),
        grid=(t // bm, MLA_HEADS // heads),
        in_specs=[pl.BlockSpec((bm, q_lora), lambda i, j: (i, c_off // q_lora)),
                  pl.BlockSpec((1, q_lora), lambda i, j: (0, 0)),
                  pl.BlockSpec((q_lora, bn), lambda i, j: (0, j)),
                  pl.BlockSpec((bm, LANES), lambda i, j: (pos(i), 0)),
                  pl.BlockSpec((bm, LANES), lambda i, j: (pos(i), 0))],
        out_specs=pl.BlockSpec((bm, bn), lambda i, j: (i, j)),
        out_shape=jax.ShapeDtypeStruct((t, MLA_HEADS * MLA_QK_PAD), BF16),
        scratch_shapes=[pltpu.VMEM((bm, q_lora), BF16)],
        compiler_params=_params(2),
        name="mla_q_proj",
    )(p, g.reshape(1, q_lora), wq, cos, sin)


def _kvproj_kernel(c_ref, kr_ref, g_ref, wk_ref, wv_ref, cos_ref, sin_ref, k_ref, v_ref):
    xn = _rms_norm(c_ref[...], g_ref[...]).astype(BF16)
    kn = jnp.dot(xn, wk_ref[...], preferred_element_type=F32)
    v_ref[...] = jnp.dot(xn, wv_ref[...], preferred_element_type=F32).astype(BF16)
    k_rope = _rope(kr_ref[...], cos_ref[...], sin_ref[...]).astype(BF16)
    for h in range(MLA_HEADS):
        lo = h * MLA_QK_PAD
        k_ref[:, lo:lo + MLA_NOPE] = kn[:, h * MLA_NOPE:(h + 1) * MLA_NOPE].astype(BF16)
        k_ref[:, lo + MLA_NOPE:lo + MLA_QK_PAD] = k_rope


def _kv_proj(p, c_off, kv_lora, kr_off, g, wk, wv, cos, sin, groups, bm):
    t = p.shape[0]
    pos = _pos_block_map(groups, bm)
    nk, nv = MLA_HEADS * MLA_QK_PAD, MLA_HEADS * MLA_V
    return pl.pallas_call(
        _kvproj_kernel,
        grid=(t // bm,),
        in_specs=[pl.BlockSpec((bm, kv_lora), lambda i: (i, c_off // kv_lora)),
                  pl.BlockSpec((bm, LANES), lambda i: (i, kr_off // LANES)),
                  pl.BlockSpec((1, kv_lora), lambda i: (0, 0)),
                  pl.BlockSpec((kv_lora, MLA_HEADS * MLA_NOPE), lambda i: (0, 0)),
                  pl.BlockSpec((kv_lora, nv), lambda i: (0, 0)),
                  pl.BlockSpec((bm, LANES), lambda i: (pos(i), 0)),
                  pl.BlockSpec((bm, LANES), lambda i: (pos(i), 0))],
        out_specs=[pl.BlockSpec((bm, nk), lambda i: (i, 0)),
                   pl.BlockSpec((bm, nv), lambda i: (i, 0))],
        out_shape=[jax.ShapeDtypeStruct((t, nk), BF16), jax.ShapeDtypeStruct((t, nv), BF16)],
        compiler_params=_params(1),
        name="mla_kv_proj",
    )(p, p, g.reshape(1, kv_lora), wk, wv, cos, sin)


def _group_call(kernel, *, grid, in_specs, out_spec, out, args, scratch_shapes=(), name):
    def body(*refs):
        n_in = len(in_specs)
        kernel(*refs[:n_in], *refs[n_in + 1:])

    return pl.pallas_call(
        body,
        grid=grid,
        in_specs=list(in_specs) + [pl.BlockSpec(memory_space=pl.ANY)],
        out_specs=out_spec,
        out_shape=jax.ShapeDtypeStruct(out.shape, out.dtype),
        input_output_aliases={len(in_specs): 0},
        scratch_shapes=list(scratch_shapes),
        compiler_params=_params(len(grid)),
        name=name,
    )(*args, out)


def _softmax_pv(s, v):
    m = jnp.max(s, -1, keepdims=True)
    e = jnp.exp(s - m)
    l = jnp.sum(e, -1, keepdims=True)
    return jnp.dot(e.astype(BF16), v, preferred_element_type=F32) / l


def _mla_attn_kernel(q_ref, k_ref, v_ref, o_ref):
    s = lax.dot_general(q_ref[...], k_ref[...], NT_DIMS, preferred_element_type=F32)
    o_ref[...] = _softmax_pv(s, v_ref[...]).astype(o_ref.dtype)


def _mla_attention(q, k, v, groups):
    t = q.shape[0]
    out = jnp.zeros((t, MLA_HEADS * MLA_V), BF16)
    for row_off, batch, seq in groups:
        bq = _block(seq, 256)
        nq = seq // bq
        q_row = lambda b, h, i, nq=nq, o=row_off // bq: (o + b * nq + i, h)
        kv_row = lambda b, h, i, o=row_off // seq: (o + b, h)
        out = _group_call(
            _mla_attn_kernel,
            grid=(batch, MLA_HEADS, nq),
            in_specs=[pl.BlockSpec((bq, MLA_QK_PAD), q_row),
                      pl.BlockSpec((seq, MLA_QK_PAD), kv_row),
                      pl.BlockSpec((seq, MLA_V), kv_row)],
            out_spec=pl.BlockSpec((bq, MLA_V), q_row),
            out=out, args=(q, k, v), name="mla_attention")
    return out


def _gla_kernel(*refs, reverse, n_sub, final):
    if final:
        (q_ref, k_ref, v_ref, lr_ref, w2_ref, gb_ref, fwd_ref, gr_ref, ng_ref,
         o_ref, st_ref) = refs
    else:
        q_ref, k_ref, v_ref, lr_ref, w2_ref, gb_ref, o_ref, st_ref = refs
    c = GLA_CHUNK

    @pl.when(pl.program_id(2) == 0)
    def _():
        st_ref[...] = jnp.zeros_like(st_ref)

    gate = jnp.dot(lr_ref[...].astype(BF16), w2_ref[...], preferred_element_type=F32) + gb_ref[...]
    log_a = (jnp.minimum(gate, 0.0) - jnp.log1p(jnp.exp(-jnp.abs(gate)))) * (1.0 / GLA_GATE_NORM)

    row = lax.broadcasted_iota(jnp.int32, (c, c), 0)
    col = lax.broadcasted_iota(jnp.int32, (c, c), 1)
    tri = (col >= row) if reverse else (col <= row)
    tri_b = tri.astype(BF16)
    q_scale = GLA_DK ** -0.5

    for ci in (reversed(range(n_sub)) if reverse else range(n_sub)):
        sl = pl.ds(ci * c, c)
        la = log_a[ci * c:(ci + 1) * c]
        hi = la.astype(BF16)
        r1 = la - hi.astype(F32)
        mid = r1.astype(BF16)
        lo = (r1 - mid.astype(F32)).astype(BF16)
        cum = (jnp.dot(tri_b, hi, preferred_element_type=F32)
               + jnp.dot(tri_b, mid, preferred_element_type=F32)
               + jnp.dot(tri_b, lo, preferred_element_type=F32))
        last = cum[0:1] if reverse else cum[c - 1:c]
        q = q_ref[sl, :] * q_scale
        k = k_ref[sl, :]
        v = v_ref[sl, :].astype(BF16)
        q_dec = (q * jnp.exp(cum)).astype(BF16)
        k_intra = (k * jnp.exp(-cum)).astype(BF16)
        k_to_end = (k * jnp.exp(last - cum)).astype(BF16)
        attn = lax.dot_general(q_dec, k_intra, NT_DIMS, preferred_element_type=F32)
        attn = jnp.where(tri, attn, 0.0).astype(BF16)
        state = st_ref[...]
        o = (jnp.dot(attn, v, preferred_element_type=F32)
             + lax.dot_general(q_dec, state.astype(BF16), NT_DIMS, preferred_element_type=F32))
        st_ref[...] = state * jnp.exp(last) + lax.dot_general(
            v, k_to_end, TN_DIMS, preferred_element_type=F32)
        if final:
            tot = fwd_ref[sl, :] + o
            gr = gr_ref[sl, :]
            o_ref[sl, :] = (_rms_norm(tot, ng_ref[...]) * (gr * _sigmoid(gr))).astype(o_ref.dtype)
        else:
            o_ref[sl, :] = o


def _gla_direction(p, offs, w2cat, gbcat, groups, *, reverse, fwd=None, norm_g=None):
    t = p.shape[0]
    final = fwd is not None
    out = jnp.zeros((t, GLA_HEADS * GLA_DV), BF16 if final else F32)
    d = 1 if reverse else 0
    for row_off, batch, seq in groups:
        lb = _block(seq, 256)
        nb = seq // lb

        def rows(b, h, n, nb=nb, o=row_off // lb):
            return o + b * nb + ((nb - 1 - n) if reverse else n)

        head_dk = lambda off: (lambda b, h, n: (rows(b, h, n), off // GLA_DK + h))
        head_dv = lambda off: (lambda b, h, n: (rows(b, h, n), off // GLA_DV + h))
        in_specs = [pl.BlockSpec((lb, GLA_DK), head_dk(offs["gq"])),
                    pl.BlockSpec((lb, GLA_DK), head_dk(offs["gk"])),
                    pl.BlockSpec((lb, GLA_DV), head_dv(offs["gv"])),
                    pl.BlockSpec((lb, LANES), lambda b, h, n: (rows(b, h, n), offs["glr"] // LANES)),
                    pl.BlockSpec((LANES, GLA_DK), lambda b, h, n: (0, d * GLA_HEADS + h)),
                    pl.BlockSpec((1, GLA_DK), lambda b, h, n: (0, d * GLA_HEADS + h))]
        args = [p, p, p, p, w2cat, gbcat]
        if final:
            in_specs += [pl.BlockSpec((lb, GLA_DV), head_dv(0)),
                         pl.BlockSpec((lb, GLA_DV), head_dv(offs["gr"])),
                         pl.BlockSpec((1, GLA_DV), lambda b, h, n: (0, 0))]
            args += [fwd, p, norm_g.reshape(1, GLA_DV)]
        out = _group_call(
            functools.partial(_gla_kernel, reverse=reverse, n_sub=lb // GLA_CHUNK, final=final),
            grid=(batch, GLA_HEADS, nb),
            in_specs=in_specs,
            out_spec=pl.BlockSpec((lb, GLA_DV), head_dv(0)),
            out=out, args=args,
            scratch_shapes=[pltpu.VMEM((GLA_DV, GLA_DK), F32)],
            name="gla_bwd" if reverse else "gla_fwd")
    return out


def _merge_kernel(a1_ref, a2_ref, w1_ref, w2_ref, g1_ref, g2_ref, b1_ref, b2_ref, o_ref):
    y1 = jnp.dot(a1_ref[...], w1_ref[...], preferred_element_type=F32)
    y2 = jnp.dot(a2_ref[...], w2_ref[...], preferred_element_type=F32)
    s1 = _sigmoid(g1_ref[...] + b1_ref[...])
    s2 = _sigmoid(g2_ref[...] + b2_ref[...])
    o_ref[...] = (s1 * y1 + s2 * y2).astype(o_ref.dtype)


def _branch_merge(o_mla, o_gla, w_mla, w_gla, p, gm_off, b_merge):
    t, k1 = o_mla.shape
    k2 = o_gla.shape[1]
    d = w_mla.shape[1]
    bm, bn = _block(t, 1024), _block(d, 512)
    g0, nd = gm_off // bn, d // bn
    return pl.pallas_call(
        _merge_kernel,
        grid=(t // bm, nd),
        in_specs=[pl.BlockSpec((bm, k1), lambda i, j: (i, 0)),
                  pl.BlockSpec((bm, k2), lambda i, j: (i, 0)),
                  pl.BlockSpec((k1, bn), lambda i, j: (0, j)),
                  pl.BlockSpec((k2, bn), lambda i, j: (0, j)),
                  pl.BlockSpec((bm, bn), lambda i, j: (i, g0 + j)),
                  pl.BlockSpec((bm, bn), lambda i, j: (i, g0 + nd + j)),
                  pl.BlockSpec((1, bn), lambda i, j: (0, j)),
                  pl.BlockSpec((1, bn), lambda i, j: (0, nd + j))],
        out_specs=pl.BlockSpec((bm, bn), lambda i, j: (i, j)),
        out_shape=jax.ShapeDtypeStruct((t, d), BF16),
        compiler_params=_params(2),
        name="branch_merge",
    )(o_mla, o_gla, w_mla, w_gla, p, p, b_merge.reshape(1, 2 * d), b_merge.reshape(1, 2 * d))


def _xattn_kernel(q_ref, k_ref, v_ref, o_ref, *, scale):
    s = lax.dot_general(q_ref[...], k_ref[...], NT_DIMS, preferred_element_type=F32) * scale
    o_ref[...] = _softmax_pv(s, v_ref[...]).astype(o_ref.dtype)


def _cross_attention(q, kv, groups, mem_groups, n_mem):
    t, d = q.shape
    hd = d // XA_HEADS
    out = jnp.zeros((t, d), BF16)
    for (row_off, batch, seq), mem_off in zip(groups, mem_groups):
        bq = _block(seq, 512)
        nq = seq // bq
        q_row = lambda b, i, h, nq=nq, o=row_off // bq: (o + b * nq + i, h)
        out = _group_call(
            functools.partial(_xattn_kernel, scale=hd ** -0.5),
            grid=(batch, nq, XA_HEADS),
            in_specs=[pl.BlockSpec((bq, hd), q_row),
                      pl.BlockSpec((n_mem, hd), lambda b, i, h, o=mem_off // n_mem: (o + b, h)),
                      pl.BlockSpec((n_mem, hd),
                                   lambda b, i, h, o=mem_off // n_mem: (o + b, XA_HEADS + h))],
            out_spec=pl.BlockSpec((bq, hd), q_row),
            out=out, args=(q, kv, kv), name="cross_attention")
    return out


def _in_proj_layout(d_model, q_lora, kv_lora):
    gla_key, gla_val = GLA_HEADS * GLA_DK, GLA_HEADS * GLA_DV
    segs = (("cq", q_lora, q_lora), ("ckv", kv_lora, kv_lora), ("krope", LANES, LANES),
            ("glr", LANES, LANES), ("gq", gla_key, GLA_DK), ("gk", gla_key, GLA_DK),
            ("gv", gla_val, GLA_DV), ("gr", gla_val, GLA_DV), ("gm", 2 * d_model, 1024))
    offs, pos = {}, 0
    for name, width, align in segs:
        pos = _round_up(pos, align)
        offs[name] = pos
        pos += width
    return offs, _round_up(pos, 1024)


def _rotate_half_cols(w):
    half = MLA_ROPE // 2
    return jnp.concatenate([-w[:, half:], w[:, :half]], axis=1)


def _pack_in_weight(w_in, offs, total, q_lora, kv_lora):
    d = w_in.shape[0]
    gla_key, gla_val = GLA_HEADS * GLA_DK, GLA_HEADS * GLA_DV
    sizes = (q_lora, kv_lora, MLA_ROPE, gla_key, gla_key, gla_val, gla_val,
             2 * GLA_GATE_RANK, 2 * d)
    src = np.cumsum((0,) + sizes)
    seg = lambda i: w_in[:, src[i]:src[i + 1]]
    pieces = {"cq": seg(0), "ckv": seg(1),
              "krope": jnp.concatenate([seg(2), _rotate_half_cols(seg(2))], axis=1),
              "glr": seg(7), "gq": seg(3), "gk": seg(4), "gv": seg(5), "gr": seg(6), "gm": seg(8)}
    cols, pos = [], 0
    for name in sorted(offs, key=offs.get):
        if offs[name] > pos:
            cols.append(jnp.zeros((d, offs[name] - pos), w_in.dtype))
        cols.append(pieces[name])
        pos = offs[name] + pieces[name].shape[1]
    if total > pos:
        cols.append(jnp.zeros((d, total - pos), w_in.dtype))
    return jnp.concatenate(cols, axis=1).astype(BF16)


def _pack_uq(w_uq):
    q_lora = w_uq.shape[0]
    w = w_uq.reshape(q_lora, MLA_HEADS, MLA_NOPE + MLA_ROPE)
    rope = w[:, :, MLA_NOPE:]
    half = MLA_ROPE // 2
    rot = jnp.concatenate([-rope[:, :, half:], rope[:, :, :half]], axis=2)
    return jnp.concatenate([w, rot], axis=2).reshape(q_lora, MLA_HEADS * MLA_QK_PAD).astype(BF16)


def _pack_ukv(w_ukv):
    kv_lora = w_ukv.shape[0]
    w = w_ukv.reshape(kv_lora, MLA_HEADS, MLA_NOPE + MLA_V)
    wk = w[:, :, :MLA_NOPE].reshape(kv_lora, MLA_HEADS * MLA_NOPE)
    wv = w[:, :, MLA_NOPE:].reshape(kv_lora, MLA_HEADS * MLA_V)
    return wk.astype(BF16), wv.astype(BF16)


def _pack_gate(w2, gb):
    r, key = w2.shape[1], w2.shape[2]
    z = jnp.zeros((r, key), w2.dtype)
    top = jnp.concatenate([w2[0], z], axis=1)
    bot = jnp.concatenate([z, w2[1]], axis=1)
    pad = jnp.zeros((LANES - 2 * r, 2 * key), w2.dtype)
    return jnp.concatenate([top, bot, pad], axis=0).astype(BF16), gb.reshape(1, 2 * key)


def kernel(x_prompt, x_sample, mem_prompt, mem_sample, ln_in_g, ln_in_b, w_in, b_merge, mla_q_norm, w_uq, mla_kv_norm, w_ukv, gla_gate_w2, gla_gate_b, gla_norm, w_branch_mla, w_branch_gla, w_mix_out, ln1_g, ln1_b, xa_wq, xa_wkv, xa_wo, ln2_g, ln2_b, mlp_w1, mlp_w2, ln3_g, ln3_b):
    assert w_in.shape[0] == DEPTH
    ba, sa, d = x_prompt.shape
    bb, sb, _ = x_sample.shape
    n_mem = mem_prompt.shape[1]
    ta, tb = ba * sa, bb * sb
    groups = ((0, ba, sa), (ta, bb, sb))
    mem_groups = (0, ba * n_mem)
    q_lora, kv_lora = mla_q_norm.shape[1], mla_kv_norm.shape[1]
    assert ta % sb == 0 and ta % 1024 == 0 and sa % 1024 == 0 and sb % 1024 == 0

    offs, n_in = _in_proj_layout(d, q_lora, kv_lora)
    w_in_p = _pack_in_weight(w_in[0], offs, n_in, q_lora, kv_lora)
    wq_p = _pack_uq(w_uq[0])
    wk_p, wv_p = _pack_ukv(w_ukv[0])
    w2cat, gbcat = _pack_gate(gla_gate_w2[0], gla_gate_b[0])
    cos, sin = _rope_tables(max(sa, sb))

    h, h_b = _ln_in(x_prompt.reshape(ta, d), x_sample.reshape(tb, d), ln_in_g, ln_in_b)

    p = _matmul(h_b, w_in_p, F32, name="in_proj")
    q = _q_proj(p, offs["cq"], q_lora, mla_q_norm[0], wq_p, cos, sin, groups, 1024)
    k, v = _kv_proj(p, offs["ckv"], kv_lora, offs["krope"], mla_kv_norm[0], wk_p, wv_p,
                    cos, sin, groups, 512)
    o_mla = _mla_attention(q, k, v, groups)
    gla_f = _gla_direction(p, offs, w2cat, gbcat, groups, reverse=False)
    o_gla = _gla_direction(p, offs, w2cat, gbcat, groups, reverse=True, fwd=gla_f,
                           norm_g=gla_norm[0])
    merged = _branch_merge(o_mla, o_gla, w_branch_mla[0].astype(BF16),
                           w_branch_gla[0].astype(BF16), p, offs["gm"], b_merge[0])
    mix = _matmul(merged, w_mix_out[0].astype(BF16), F32, name="mix_out")
    h, h_b = _res_ln(h, mix, ln1_g[0], ln1_b[0], (F32, BF16))

    mem = jnp.concatenate([mem_prompt.reshape(ba * n_mem, d), mem_sample.reshape(bb * n_mem, d)])
    xq = _matmul(h_b, xa_wq[0].astype(BF16), BF16, name="xa_q")
    xkv = _matmul(mem.astype(BF16), xa_wkv[0].astype(BF16), BF16, name="xa_kv")
    o_x = _cross_attention(xq, xkv, groups, mem_groups, n_mem)
    xo = _matmul(o_x, xa_wo[0].astype(BF16), F32, name="xa_o")
    h, h_b = _res_ln(h, xo, ln2_g[0], ln2_b[0], (F32, BF16))

    u = _matmul(h_b, mlp_w1[0].astype(BF16), BF16, relu2=True, name="mlp_up")
    ff = _matmul(u, mlp_w2[0].astype(BF16), F32, name="mlp_down")
    y_a, = _res_ln(h, ff, ln3_g[0], ln3_b[0], (F32,), row_off=0, rows=ta)
    y_b, = _res_ln(h, ff, ln3_g[0], ln3_b[0], (F32,), row_off=ta, rows=tb)
    return y_a.reshape(ba, sa, d), y_b.reshape(bb, sb, d)
```

```python
import functools

import numpy as np
import jax
import jax.numpy as jnp
from jax import lax
from jax.experimental import pallas as pl
from jax.experimental.pallas import tpu as pltpu

MLA_HEADS = 16
MLA_NOPE = 128
MLA_ROPE = 64
MLA_V = 128
ROPE_THETA = 10000.0
GLA_HEADS = 4
GLA_DK = 256
GLA_DV = 512
GLA_GATE_RANK = 16
GLA_GATE_NORM = 16.0
GLA_CHUNK = 64
XA_HEADS = 4
LN_EPS = 1e-5
RMS_EPS = 1e-6
DEPTH = 1
DN_ALPHA = (2.0 * DEPTH) ** 0.25

LANES = 128
MLA_QK_PAD = 2 * LANES
MLA_V_PAD = 2 * LANES
MLA_Q_ROWS = 256
VMEM_LIMIT_BYTES = 56 * 2**20

BF16 = jnp.bfloat16
F32 = jnp.float32
NT_DIMS = (((1,), (1,)), ((), ()))
TN_DIMS = (((0,), (0,)), ((), ()))


def _params(n_grid):
    return pltpu.CompilerParams(dimension_semantics=("arbitrary",) * n_grid,
                                vmem_limit_bytes=VMEM_LIMIT_BYTES)


def _block(n, pref):
    b = min(n, pref)
    while n % b:
        b //= 2
    return b


def _round_up(n, m):
    return -(-n // m) * m


def _layer_norm(x, g, b):
    mu = jnp.mean(x, -1, keepdims=True)
    xc = x - mu
    var = jnp.mean(xc * xc, -1, keepdims=True)
    return xc * lax.rsqrt(var + LN_EPS) * g + b


def _rms_norm(x, g):
    return x * lax.rsqrt(jnp.mean(x * x, -1, keepdims=True) + RMS_EPS) * g


def _sigmoid(x):
    return 1.0 / (1.0 + jnp.exp(-x))


def _rope(x, cos, sin):
    return x * cos + pltpu.roll(x, MLA_ROPE, 1) * sin


def _rope_rotate(x, cos, sin):
    half = MLA_ROPE // 2
    lane = lax.broadcasted_iota(jnp.int32, x.shape, 1)
    rot = jnp.where(lane < half, -pltpu.roll(x, LANES - half, 1), pltpu.roll(x, half, 1))
    return x * cos + rot * sin


def _ln_in_kernel(xa_ref, xb_ref, g_ref, b_ref, h_ref, hb_ref, *, n_a):
    def emit(x_ref):
        y = _layer_norm(x_ref[...], g_ref[...], b_ref[...])
        h_ref[...] = y
        hb_ref[...] = y.astype(BF16)

    @pl.when(pl.program_id(0) < n_a)
    def _():
        emit(xa_ref)

    @pl.when(pl.program_id(0) >= n_a)
    def _():
        emit(xb_ref)


def _ln_in(xa, xb, g, b):
    ta, d = xa.shape
    tb = xb.shape[0]
    bm = _block(np.gcd(ta, tb), 256)
    n_a, n_b = ta // bm, tb // bm
    row = pl.BlockSpec((1, d), lambda i: (0, 0))
    out = pl.BlockSpec((bm, d), lambda i: (i, 0))
    return pl.pallas_call(
        functools.partial(_ln_in_kernel, n_a=n_a),
        grid=(n_a + n_b,),
        in_specs=[pl.BlockSpec((bm, d), lambda i: (jnp.minimum(i, n_a - 1), 0)),
                  pl.BlockSpec((bm, d), lambda i: (jnp.maximum(i - n_a, 0), 0)),
                  row, row],
        out_specs=[out, out],
        out_shape=[jax.ShapeDtypeStruct((ta + tb, d), F32),
                   jax.ShapeDtypeStruct((ta + tb, d), BF16)],
        compiler_params=_params(1),
        name="ln_in",
    )(xa, xb, g.reshape(1, d), b.reshape(1, d))


def _res_ln_kernel(h_ref, y_ref, g_ref, b_ref, *o_refs):
    z = _layer_norm(DN_ALPHA * h_ref[...] + y_ref[...], g_ref[...], b_ref[...])
    for o_ref in o_refs:
        o_ref[...] = z.astype(o_ref.dtype)


def _res_ln(h, y, g, b, out_dtypes, *, row_off=0, rows=None):
    t, d = h.shape
    rows = t if rows is None else rows
    bm = _block(np.gcd(rows, row_off) if row_off else rows, 256)
    off = row_off // bm
    row = pl.BlockSpec((1, d), lambda i: (0, 0))
    src = pl.BlockSpec((bm, d), lambda i: (i + off, 0))
    dst = pl.BlockSpec((bm, d), lambda i: (i, 0))
    return pl.pallas_call(
        _res_ln_kernel,
        grid=(rows // bm,),
        in_specs=[src, src, row, row],
        out_specs=[dst] * len(out_dtypes),
        out_shape=[jax.ShapeDtypeStruct((rows, d), dt) for dt in out_dtypes],
        compiler_params=_params(1),
        name="res_ln",
    )(h, y, g.reshape(1, d), b.reshape(1, d))


def _matmul_kernel(a_ref, b_ref, o_ref, *scratch, nk, relu2):
    part = jnp.dot(a_ref[...], b_ref[...], preferred_element_type=F32)

    def finish(acc):
        if relu2:
            acc = jnp.square(jnp.maximum(acc, 0.0))
        o_ref[...] = acc.astype(o_ref.dtype)

    if nk == 1:
        finish(part)
        return
    acc_ref, = scratch
    k = pl.program_id(2)

    @pl.when(k == 0)
    def _():
        acc_ref[...] = part

    @pl.when(jnp.logical_and(k > 0, k < nk - 1))
    def _():
        acc_ref[...] += part

    @pl.when(k == nk - 1)
    def _():
        finish(acc_ref[...] + part)


def _matmul(a, b, out_dtype, *, bm=1024, bn=1024, bk=4096, relu2=False, name="matmul"):
    m, kdim = a.shape
    n = b.shape[1]
    bm, bn, bk = _block(m, bm), _block(n, bn), _block(kdim, bk)
    nk = kdim // bk
    return pl.pallas_call(
        functools.partial(_matmul_kernel, nk=nk, relu2=relu2),
        grid=(m // bm, n // bn, nk),
        in_specs=[pl.BlockSpec((bm, bk), lambda i, j, k: (i, k)),
                  pl.BlockSpec((bk, bn), lambda i, j, k: (k, j))],
        out_specs=pl.BlockSpec((bm, bn), lambda i, j, k: (i, j)),
        out_shape=jax.ShapeDtypeStruct((m, n), out_dtype),
        scratch_shapes=[pltpu.VMEM((bm, bn), F32)] if nk > 1 else [],
        compiler_params=_params(3),
        name=name,
    )(a, b)


def _matmul_wres_kernel(a_ref, w_ref, o_ref, wb_ref, *, relu2):
    @pl.when(pl.program_id(1) == 0)
    def _():
        wb_ref[...] = w_ref[...].astype(BF16)

    acc = jnp.dot(a_ref[...], wb_ref[...], preferred_element_type=F32)
    if relu2:
        acc = jnp.square(jnp.maximum(acc, 0.0))
    o_ref[...] = acc.astype(o_ref.dtype)


def _matmul_wres(a, w, out_dtype, *, bm=1024, bn=512, relu2=False, name="matmul_wres"):
    m, kdim = a.shape
    n = w.shape[1]
    bm, bn = _block(m, bm), _block(n, bn)
    return pl.pallas_call(
        functools.partial(_matmul_wres_kernel, relu2=relu2),
        grid=(n // bn, m // bm),
        in_specs=[pl.BlockSpec((bm, kdim), lambda j, i: (i, 0)),
                  pl.BlockSpec((kdim, bn), lambda j, i: (0, j))],
        out_specs=pl.BlockSpec((bm, bn), lambda j, i: (i, j)),
        out_shape=jax.ShapeDtypeStruct((m, n), out_dtype),
        scratch_shapes=[pltpu.VMEM((kdim, bn), BF16)],
        compiler_params=_params(2),
        name=name,
    )(a, w)


def _rope_tables(seq_len):
    half = MLA_ROPE // 2
    inv = 1.0 / (ROPE_THETA ** (jnp.arange(0, MLA_ROPE, 2, dtype=F32) / MLA_ROPE))
    ang = jnp.arange(seq_len, dtype=F32)[:, None] * inv[None, :]
    zero = jnp.zeros((seq_len, LANES - 2 * half), F32)
    cos = jnp.concatenate([jnp.cos(ang), jnp.cos(ang), zero], axis=1)
    sin = jnp.concatenate([jnp.sin(ang), jnp.sin(ang), zero], axis=1)
    return cos, sin


def _pos_block_map(groups, bm):
    (_, _, s_a), (off_b, _, s_b) = groups
    n_a = off_b // bm

    def index(i):
        return jnp.where(i < n_a, i % (s_a // bm), (i - n_a) % (s_b // bm))
    return index


def _qproj_kernel(c_ref, g_ref, w_ref, cos_ref, sin_ref, o_ref, xn_ref, *, heads, scale):
    @pl.when(pl.program_id(1) == 0)
    def _():
        xn_ref[...] = _rms_norm(c_ref[...], g_ref[...]).astype(BF16)

    r = jnp.dot(xn_ref[...], w_ref[...], preferred_element_type=F32)
    cos, sin = cos_ref[...], sin_ref[...]
    for h in range(heads):
        lo = h * MLA_QK_PAD
        o_ref[:, lo:lo + MLA_NOPE] = (r[:, lo:lo + MLA_NOPE] * scale).astype(BF16)
        o_ref[:, lo + MLA_NOPE:lo + MLA_QK_PAD] = (
            _rope(r[:, lo + MLA_NOPE:lo + MLA_QK_PAD], cos, sin) * scale).astype(BF16)


def _q_proj(p, c_off, q_lora, g, wq, cos, sin, groups, bm):
    t = p.shape[0]
    heads = min(4, MLA_HEADS)
    bn = heads * MLA_QK_PAD
    pos = _pos_block_map(groups, bm)
    scale = (MLA_NOPE + MLA_ROPE) ** -0.5 * np.log2(np.e)
    return pl.pallas_call(
        functools.partial(_qproj_kernel, heads=heads, scale=scale),
        grid=(t // bm, MLA_HEADS // heads),
        in_specs=[pl.BlockSpec((bm, q_lora), lambda i, j: (i, c_off // q_lora)),
                  pl.BlockSpec((1, q_lora), lambda i, j: (0, 0)),
                  pl.BlockSpec((q_lora, bn), lambda i, j: (0, j)),
                  pl.BlockSpec((bm, LANES), lambda i, j: (pos(i), 0)),
                  pl.BlockSpec((bm, LANES), lambda i, j: (pos(i), 0))],
        out_specs=pl.BlockSpec((bm, bn), lambda i, j: (i, j)),
        out_shape=jax.ShapeDtypeStruct((t, MLA_HEADS * MLA_QK_PAD), BF16),
        scratch_shapes=[pltpu.VMEM((bm, q_lora), BF16)],
        compiler_params=_params(2),
        name="mla_q_proj",
    )(p, g.reshape(1, q_lora), wq, cos, sin)


def _kvproj_kernel(c_ref, kr_ref, g_ref, wk_ref, wv_ref, cos_ref, sin_ref, k_ref, v_ref):
    xn = _rms_norm(c_ref[...], g_ref[...]).astype(BF16)
    kn = jnp.dot(xn, wk_ref[...], preferred_element_type=F32)
    vn = jnp.dot(xn, wv_ref[...], preferred_element_type=F32)
    k_rope = _rope_rotate(kr_ref[...], cos_ref[...], sin_ref[...]).astype(BF16)
    ones = jnp.ones((k_ref.shape[0], MLA_V_PAD - MLA_V), BF16)
    for h in range(MLA_HEADS):
        lo = h * MLA_QK_PAD
        k_ref[:, lo:lo + MLA_NOPE] = kn[:, h * MLA_NOPE:(h + 1) * MLA_NOPE].astype(BF16)
        k_ref[:, lo + MLA_NOPE:lo + MLA_QK_PAD] = k_rope
        lo = h * MLA_V_PAD
        v_ref[:, lo:lo + MLA_V] = vn[:, h * MLA_V:(h + 1) * MLA_V].astype(BF16)
        v_ref[:, lo + MLA_V:lo + MLA_V_PAD] = ones


def _kv_proj(p, c_off, kv_lora, kr_off, g, wk, wv, cos, sin, groups, bm):
    t = p.shape[0]
    pos = _pos_block_map(groups, bm)
    nk, nv = MLA_HEADS * MLA_QK_PAD, MLA_HEADS * MLA_V
    return pl.pallas_call(
        _kvproj_kernel,
        grid=(t // bm,),
        in_specs=[pl.BlockSpec((bm, kv_lora), lambda i: (i, c_off // kv_lora)),
                  pl.BlockSpec((bm, LANES), lambda i: (i, kr_off // LANES)),
                  pl.BlockSpec((1, kv_lora), lambda i: (0, 0)),
                  pl.BlockSpec((kv_lora, MLA_HEADS * MLA_NOPE), lambda i: (0, 0)),
                  pl.BlockSpec((kv_lora, nv), lambda i: (0, 0)),
                  pl.BlockSpec((bm, LANES), lambda i: (pos(i), 0)),
                  pl.BlockSpec((bm, LANES), lambda i: (pos(i), 0))],
        out_specs=[pl.BlockSpec((bm, nk), lambda i: (i, 0)),
                   pl.BlockSpec((bm, MLA_HEADS * MLA_V_PAD), lambda i: (i, 0))],
        out_shape=[jax.ShapeDtypeStruct((t, nk), BF16),
                   jax.ShapeDtypeStruct((t, MLA_HEADS * MLA_V_PAD), BF16)],
        compiler_params=_params(1),
        name="mla_kv_proj",
    )(p, p, g.reshape(1, kv_lora), wk, wv, cos, sin)


def _group_call(kernel, *, grid, in_specs, out_spec, out, args, scratch_shapes=(), name):
    def body(*refs):
        n_in = len(in_specs)
        kernel(*refs[:n_in], *refs[n_in + 1:])

    return pl.pallas_call(
        body,
        grid=grid,
        in_specs=list(in_specs) + [pl.BlockSpec(memory_space=pl.ANY)],
        out_specs=out_spec,
        out_shape=jax.ShapeDtypeStruct(out.shape, out.dtype),
        input_output_aliases={len(in_specs): 0},
        scratch_shapes=list(scratch_shapes),
        compiler_params=_params(len(grid)),
        name=name,
    )(*args, out)


def _softmax_pv(s, v):
    m = jnp.max(s, -1, keepdims=True)
    e = jnp.exp(s - m)
    l = jnp.sum(e, -1, keepdims=True)
    return jnp.dot(e.astype(BF16), v, preferred_element_type=F32) / l


def _mla_attn_kernel(q_ref, k_ref, v_ref, o_ref, *, chains):
    for c in range(chains):
        rows = pl.ds(c * MLA_Q_ROWS, MLA_Q_ROWS)
        s = lax.dot_general(q_ref[rows, :], k_ref[...], NT_DIMS, preferred_element_type=F32)
        e = jnp.exp2(s - jnp.max(s, -1, keepdims=True)).astype(BF16)
        acc = jnp.dot(e, v_ref[...], preferred_element_type=F32)
        o_ref[rows, :] = (acc[:, :MLA_V] / acc[:, MLA_V:]).astype(o_ref.dtype)


def _mla_attention(q, k, v, groups):
    t = q.shape[0]
    out = jnp.zeros((t, MLA_HEADS * MLA_V), BF16)
    for row_off, batch, seq in groups:
        bq = _block(seq, 8 * MLA_Q_ROWS)
        nq = seq // bq
        q_row = lambda b, h, i, nq=nq, o=row_off // bq: (o + b * nq + i, h)
        kv_row = lambda b, h, i, o=row_off // seq: (o + b, h)
        out = _group_call(
            functools.partial(_mla_attn_kernel, chains=bq // MLA_Q_ROWS),
            grid=(batch, MLA_HEADS, nq),
            in_specs=[pl.BlockSpec((bq, MLA_QK_PAD), q_row),
                      pl.BlockSpec((seq, MLA_QK_PAD), kv_row),
                      pl.BlockSpec((seq, MLA_V_PAD), kv_row)],
            out_spec=pl.BlockSpec((bq, MLA_V), q_row),
            out=out, args=(q, k, v), name="mla_attention")
    return out


def _gla_kernel(*refs, reverse, n_sub, final):
    if final:
        (q_ref, k_ref, v_ref, lr_ref, w2_ref, gb_ref, fwd_ref, gr_ref, ng_ref,
         o_ref, st_ref) = refs
    else:
        q_ref, k_ref, v_ref, lr_ref, w2_ref, gb_ref, o_ref, st_ref = refs
    c = GLA_CHUNK

    @pl.when(pl.program_id(2) == 0)
    def _():
        st_ref[...] = jnp.zeros_like(st_ref)

    gate = jnp.dot(lr_ref[...].astype(BF16), w2_ref[...], preferred_element_type=F32) + gb_ref[...]
    log_a = (jnp.minimum(gate, 0.0) - jnp.log1p(jnp.exp(-jnp.abs(gate)))) * (1.0 / GLA_GATE_NORM)

    row = lax.broadcasted_iota(jnp.int32, (c, c), 0)
    col = lax.broadcasted_iota(jnp.int32, (c, c), 1)
    tri = (col >= row) if reverse else (col <= row)
    tri_b = tri.astype(BF16)
    q_scale = GLA_DK ** -0.5

    for ci in (reversed(range(n_sub)) if reverse else range(n_sub)):
        sl = pl.ds(ci * c, c)
        la = log_a[ci * c:(ci + 1) * c]
        hi = la.astype(BF16)
        r1 = la - hi.astype(F32)
        mid = r1.astype(BF16)
        lo = (r1 - mid.astype(F32)).astype(BF16)
        cum = (jnp.dot(tri_b, hi, preferred_element_type=F32)
               + jnp.dot(tri_b, mid, preferred_element_type=F32)
               + jnp.dot(tri_b, lo, preferred_element_type=F32))
        last = cum[0:1] if reverse else cum[c - 1:c]
        q = q_ref[sl, :] * q_scale
        k = k_ref[sl, :]
        v = v_ref[sl, :].astype(BF16)
        q_dec = (q * jnp.exp(cum)).astype(BF16)
        k_intra = (k * jnp.exp(-cum)).astype(BF16)
        k_to_end = (k * jnp.exp(last - cum)).astype(BF16)
        attn = lax.dot_general(q_dec, k_intra, NT_DIMS, preferred_element_type=F32)
        attn = jnp.where(tri, attn, 0.0).astype(BF16)
        state = st_ref[...]
        o = (jnp.dot(attn, v, preferred_element_type=F32)
             + lax.dot_general(q_dec, state.astype(BF16), NT_DIMS, preferred_element_type=F32))
        st_ref[...] = state * jnp.exp(last) + lax.dot_general(
            v, k_to_end, TN_DIMS, preferred_element_type=F32)
        if final:
            tot = fwd_ref[sl, :] + o
            gr = gr_ref[sl, :]
            o_ref[sl, :] = (_rms_norm(tot, ng_ref[...]) * (gr * _sigmoid(gr))).astype(o_ref.dtype)
        else:
            o_ref[sl, :] = o


def _gla_direction(p, offs, w2cat, gbcat, groups, *, reverse, fwd=None, norm_g=None):
    t = p.shape[0]
    final = fwd is not None
    out = jnp.zeros((t, GLA_HEADS * GLA_DV), BF16 if final else F32)
    d = 1 if reverse else 0
    for row_off, batch, seq in groups:
        lb = _block(seq, 256)
        nb = seq // lb

        def rows(b, h, n, nb=nb, o=row_off // lb):
            return o + b * nb + ((nb - 1 - n) if reverse else n)

        head_dk = lambda off: (lambda b, h, n: (rows(b, h, n), off // GLA_DK + h))
        head_dv = lambda off: (lambda b, h, n: (rows(b, h, n), off // GLA_DV + h))
        in_specs = [pl.BlockSpec((lb, GLA_DK), head_dk(offs["gq"])),
                    pl.BlockSpec((lb, GLA_DK), head_dk(offs["gk"])),
                    pl.BlockSpec((lb, GLA_DV), head_dv(offs["gv"])),
                    pl.BlockSpec((lb, LANES), lambda b, h, n: (rows(b, h, n), offs["glr"] // LANES)),
                    pl.BlockSpec((LANES, GLA_DK), lambda b, h, n: (0, d * GLA_HEADS + h)),
                    pl.BlockSpec((1, GLA_DK), lambda b, h, n: (0, d * GLA_HEADS + h))]
        args = [p, p, p, p, w2cat, gbcat]
        if final:
            in_specs += [pl.BlockSpec((lb, GLA_DV), head_dv(0)),
                         pl.BlockSpec((lb, GLA_DV), head_dv(offs["gr"])),
                         pl.BlockSpec((1, GLA_DV), lambda b, h, n: (0, 0))]
            args += [fwd, p, norm_g.reshape(1, GLA_DV)]
        out = _group_call(
            functools.partial(_gla_kernel, reverse=reverse, n_sub=lb // GLA_CHUNK, final=final),
            grid=(batch, GLA_HEADS, nb),
            in_specs=in_specs,
            out_spec=pl.BlockSpec((lb, GLA_DV), head_dv(0)),
            out=out, args=args,
            scratch_shapes=[pltpu.VMEM((GLA_DV, GLA_DK), F32)],
            name="gla_bwd" if reverse else "gla_fwd")
    return out


def _merge_kernel(a1_ref, a2_ref, w1_ref, w2_ref, g1_ref, g2_ref, b1_ref, b2_ref, o_ref,
                  w1b_ref, w2b_ref):
    @pl.when(pl.program_id(1) == 0)
    def _():
        w1b_ref[...] = w1_ref[...].astype(BF16)
        w2b_ref[...] = w2_ref[...].astype(BF16)

    y1 = jnp.dot(a1_ref[...], w1b_ref[...], preferred_element_type=F32)
    y2 = jnp.dot(a2_ref[...], w2b_ref[...], preferred_element_type=F32)
    s1 = _sigmoid(g1_ref[...] + b1_ref[...])
    s2 = _sigmoid(g2_ref[...] + b2_ref[...])
    o_ref[...] = (s1 * y1 + s2 * y2).astype(o_ref.dtype)


def _branch_merge(o_mla, o_gla, w_mla, w_gla, p, gm_off, b_merge):
    t, k1 = o_mla.shape
    k2 = o_gla.shape[1]
    d = w_mla.shape[1]
    bm, bn = _block(t, 1024), _block(d, 512)
    g0, nd = gm_off // bn, d // bn
    return pl.pallas_call(
        _merge_kernel,
        grid=(nd, t // bm),
        in_specs=[pl.BlockSpec((bm, k1), lambda j, i: (i, 0)),
                  pl.BlockSpec((bm, k2), lambda j, i: (i, 0)),
                  pl.BlockSpec((k1, bn), lambda j, i: (0, j)),
                  pl.BlockSpec((k2, bn), lambda j, i: (0, j)),
                  pl.BlockSpec((bm, bn), lambda j, i: (i, g0 + j)),
                  pl.BlockSpec((bm, bn), lambda j, i: (i, g0 + nd + j)),
                  pl.BlockSpec((1, bn), lambda j, i: (0, j)),
                  pl.BlockSpec((1, bn), lambda j, i: (0, nd + j))],
        out_specs=pl.BlockSpec((bm, bn), lambda j, i: (i, j)),
        out_shape=jax.ShapeDtypeStruct((t, d), BF16),
        scratch_shapes=[pltpu.VMEM((k1, bn), BF16), pltpu.VMEM((k2, bn), BF16)],
        compiler_params=_params(2),
        name="branch_merge",
    )(o_mla, o_gla, w_mla, w_gla, p, p, b_merge.reshape(1, 2 * d), b_merge.reshape(1, 2 * d))


def _xattn_kernel(q_ref, k_ref, v_ref, o_ref, *, scale):
    s = lax.dot_general(q_ref[...], k_ref[...], NT_DIMS, preferred_element_type=F32) * scale
    o_ref[...] = _softmax_pv(s, v_ref[...]).astype(o_ref.dtype)


def _cross_attention(q, kv, groups, mem_groups, n_mem):
    t, d = q.shape
    hd = d // XA_HEADS
    out = jnp.zeros((t, d), BF16)
    for (row_off, batch, seq), mem_off in zip(groups, mem_groups):
        bq = _block(seq, 512)
        nq = seq // bq
        q_row = lambda b, i, h, nq=nq, o=row_off // bq: (o + b * nq + i, h)
        out = _group_call(
            functools.partial(_xattn_kernel, scale=hd ** -0.5),
            grid=(batch, nq, XA_HEADS),
            in_specs=[pl.BlockSpec((bq, hd), q_row),
                      pl.BlockSpec((n_mem, hd), lambda b, i, h, o=mem_off // n_mem: (o + b, h)),
                      pl.BlockSpec((n_mem, hd),
                                   lambda b, i, h, o=mem_off // n_mem: (o + b, XA_HEADS + h))],
            out_spec=pl.BlockSpec((bq, hd), q_row),
            out=out, args=(q, kv, kv), name="cross_attention")
    return out


PACK_COLS = 512
GATE_LANE = 64
GM_SHIFT = 96


def _in_proj_layout(d_model, q_lora, kv_lora):
    gla_key, gla_val = GLA_HEADS * GLA_DK, GLA_HEADS * GLA_DV
    s_kr = q_lora + kv_lora
    assert MLA_ROPE == GATE_LANE and 2 * GLA_GATE_RANK == GM_SHIFT - GATE_LANE
    assert s_kr % PACK_COLS == 0 and (2 * gla_key + 2 * gla_val) % PACK_COLS == 0
    assert (2 * d_model) % PACK_COLS == 0 and q_lora % kv_lora == 0
    n_a = s_kr // PACK_COLS
    n_b = (2 * gla_key + 2 * gla_val) // PACK_COLS
    n_c = 2 * d_model // PACK_COLS
    gq = PACK_COLS * (n_a + 1)
    offs = {"cq": 0, "ckv": q_lora, "krope": s_kr, "glr": s_kr + LANES, "gq": gq,
            "gk": gq + gla_key, "gv": gq + 2 * gla_key, "gr": gq + 2 * gla_key + gla_val,
            "gm": PACK_COLS * (n_a + 1 + n_b)}
    return offs, PACK_COLS * (n_a + 1 + n_b + n_c), (n_a, n_b, n_c)


def _pack_in_kernel(main_ref, extra_ref, o_ref, *, n_a, n_b):
    j = pl.program_id(1)

    @pl.when(j < n_a)
    def _():
        o_ref[...] = main_ref[...].astype(BF16)

    @pl.when(j == n_a)
    def _():
        o_ref[:, :LANES] = main_ref[:, :LANES].astype(BF16)
        o_ref[:, LANES:2 * LANES] = extra_ref[...].astype(BF16)
        o_ref[:, 2 * LANES:] = jnp.zeros((o_ref.shape[0], PACK_COLS - 2 * LANES), BF16)

    def shifted(shift):
        x = jnp.concatenate([main_ref[...], extra_ref[...]], axis=1)
        o_ref[...] = x[:, shift:shift + PACK_COLS].astype(BF16)

    @pl.when(jnp.logical_and(j > n_a, j <= n_a + n_b))
    def _():
        shifted(MLA_ROPE)

    @pl.when(j > n_a + n_b)
    def _():
        shifted(GM_SHIFT)


def _pack_in_weight(w_in, total, blocks):
    d = w_in.shape[1]
    n_a, n_b, n_c = blocks
    rows = _block(d, 1024)
    ratio = PACK_COLS // LANES
    main = lambda i, j: (0, i, jnp.where(j <= n_a, j, j - 1))
    extra = lambda i, j: (0, i, jnp.where(j == n_a, ratio * (n_a + n_b), ratio * j))
    return pl.pallas_call(
        functools.partial(_pack_in_kernel, n_a=n_a, n_b=n_b),
        grid=(d // rows, n_a + 1 + n_b + n_c),
        in_specs=[pl.BlockSpec((None, rows, PACK_COLS), main),
                  pl.BlockSpec((None, rows, LANES), extra)],
        out_specs=pl.BlockSpec((rows, PACK_COLS), lambda i, j: (i, j)),
        out_shape=jax.ShapeDtypeStruct((d, total), BF16),
        compiler_params=_params(2),
        name="pack_w_in",
    )(w_in, w_in)


def _pack_uq(w_uq):
    q_lora = w_uq.shape[0]
    w = w_uq.reshape(q_lora, MLA_HEADS, MLA_NOPE + MLA_ROPE)
    rope = w[:, :, MLA_NOPE:]
    half = MLA_ROPE // 2
    rot = jnp.concatenate([-rope[:, :, half:], rope[:, :, :half]], axis=2)
    return jnp.concatenate([w, rot], axis=2).reshape(q_lora, MLA_HEADS * MLA_QK_PAD).astype(BF16)


def _pack_ukv(w_ukv):
    kv_lora = w_ukv.shape[0]
    w = w_ukv.reshape(kv_lora, MLA_HEADS, MLA_NOPE + MLA_V)
    wk = w[:, :, :MLA_NOPE].reshape(kv_lora, MLA_HEADS * MLA_NOPE)
    wv = w[:, :, MLA_NOPE:].reshape(kv_lora, MLA_HEADS * MLA_V)
    return wk.astype(BF16), wv.astype(BF16)


def _pack_gate(w2, gb):
    r, key = w2.shape[1], w2.shape[2]
    z = jnp.zeros((r, key), w2.dtype)
    top = jnp.concatenate([w2[0], z], axis=1)
    bot = jnp.concatenate([z, w2[1]], axis=1)
    head = jnp.zeros((GATE_LANE, 2 * key), w2.dtype)
    tail = jnp.zeros((LANES - GATE_LANE - 2 * r, 2 * key), w2.dtype)
    return (jnp.concatenate([head, top, bot, tail], axis=0).astype(BF16),
            gb.reshape(1, 2 * key))


def kernel(x_prompt, x_sample, mem_prompt, mem_sample, ln_in_g, ln_in_b, w_in, b_merge, mla_q_norm, w_uq, mla_kv_norm, w_ukv, gla_gate_w2, gla_gate_b, gla_norm, w_branch_mla, w_branch_gla, w_mix_out, ln1_g, ln1_b, xa_wq, xa_wkv, xa_wo, ln2_g, ln2_b, mlp_w1, mlp_w2, ln3_g, ln3_b):
    assert w_in.shape[0] == DEPTH
    ba, sa, d = x_prompt.shape
    bb, sb, _ = x_sample.shape
    n_mem = mem_prompt.shape[1]
    ta, tb = ba * sa, bb * sb
    groups = ((0, ba, sa), (ta, bb, sb))
    mem_groups = (0, ba * n_mem)
    q_lora, kv_lora = mla_q_norm.shape[1], mla_kv_norm.shape[1]
    assert ta % sb == 0 and ta % 1024 == 0 and sa % 1024 == 0 and sb % 1024 == 0

    offs, n_in, in_blocks = _in_proj_layout(d, q_lora, kv_lora)
    w_in_p = _pack_in_weight(w_in, n_in, in_blocks)
    wq_p = _pack_uq(w_uq[0])
    wk_p, wv_p = _pack_ukv(w_ukv[0])
    w2cat, gbcat = _pack_gate(gla_gate_w2[0], gla_gate_b[0])
    cos, sin = _rope_tables(max(sa, sb))

    h, h_b = _ln_in(x_prompt.reshape(ta, d), x_sample.reshape(tb, d), ln_in_g, ln_in_b)

    p = _matmul(h_b, w_in_p, F32, name="in_proj")
    q = _q_proj(p, offs["cq"], q_lora, mla_q_norm[0], wq_p, cos, sin, groups, 1024)
    k, v = _kv_proj(p, offs["ckv"], kv_lora, offs["krope"], mla_kv_norm[0], wk_p, wv_p,
                    cos, sin, groups, 512)
    o_mla = _mla_attention(q, k, v, groups)
    gla_f = _gla_direction(p, offs, w2cat, gbcat, groups, reverse=False)
    o_gla = _gla_direction(p, offs, w2cat, gbcat, groups, reverse=True, fwd=gla_f,
                           norm_g=gla_norm[0])
    merged = _branch_merge(o_mla, o_gla, w_branch_mla[0], w_branch_gla[0], p, offs["gm"],
                           b_merge[0])
    mix = _matmul_wres(merged, w_mix_out[0], F32, name="mix_out")
    h, h_b = _res_ln(h, mix, ln1_g[0], ln1_b[0], (F32, BF16))

    mem = jnp.concatenate([mem_prompt.reshape(ba * n_mem, d), mem_sample.reshape(bb * n_mem, d)])
    xq = _matmul_wres(h_b, xa_wq[0], BF16, name="xa_q")
    xkv = _matmul_wres(mem.astype(BF16), xa_wkv[0], BF16, bm=2048, name="xa_kv")
    o_x = _cross_attention(xq, xkv, groups, mem_groups, n_mem)
    xo = _matmul_wres(o_x, xa_wo[0], F32, name="xa_o")
    h, h_b = _res_ln(h, xo, ln2_g[0], ln2_b[0], (F32, BF16))

    u = _matmul_wres(h_b, mlp_w1[0], BF16, relu2=True, name="mlp_up")
    ff = _matmul(u, mlp_w2[0].astype(BF16), F32, name="mlp_down")
    y_a, = _res_ln(h, ff, ln3_g[0], ln3_b[0], (F32,), row_off=0, rows=ta)
    y_b, = _res_ln(h, ff, ln3_g[0], ln3_b[0], (F32,), row_off=ta, rows=tb)
    return y_a.reshape(ba, sa, d), y_b.reshape(bb, sb, d)
```

```python
import functools

import numpy as np
import jax
import jax.numpy as jnp
from jax import lax
from jax.experimental import pallas as pl
from jax.experimental.pallas import tpu as pltpu

MLA_HEADS = 16
MLA_NOPE = 128
MLA_ROPE = 64
MLA_V = 128
ROPE_THETA = 10000.0
GLA_HEADS = 4
GLA_DK = 256
GLA_DV = 512
GLA_GATE_RANK = 16
GLA_GATE_NORM = 16.0
GLA_CHUNK = 64
XA_HEADS = 4
LN_EPS = 1e-5
RMS_EPS = 1e-6
DEPTH = 1
DN_ALPHA = (2.0 * DEPTH) ** 0.25

LANES = 128
MLA_QK_PAD = 2 * LANES
MLA_V_PAD = 2 * LANES
MLA_Q_ROWS = 256
VMEM_LIMIT_BYTES = 56 * 2**20

BF16 = jnp.bfloat16
F32 = jnp.float32
NT_DIMS = (((1,), (1,)), ((), ()))
TN_DIMS = (((0,), (0,)), ((), ()))


def _params(n_grid):
    return pltpu.CompilerParams(dimension_semantics=("arbitrary",) * n_grid,
                                vmem_limit_bytes=VMEM_LIMIT_BYTES)


def _block(n, pref):
    b = min(n, pref)
    while n % b:
        b //= 2
    return b


def _round_up(n, m):
    return -(-n // m) * m


def _layer_norm(x, g, b):
    mu = jnp.mean(x, -1, keepdims=True)
    xc = x - mu
    var = jnp.mean(xc * xc, -1, keepdims=True)
    return xc * lax.rsqrt(var + LN_EPS) * g + b


def _rms_norm(x, g):
    return x * lax.rsqrt(jnp.mean(x * x, -1, keepdims=True) + RMS_EPS) * g


def _sigmoid(x):
    return 1.0 / (1.0 + jnp.exp(-x))


def _rope(x, cos, sin):
    return x * cos + pltpu.roll(x, MLA_ROPE, 1) * sin


def _rope_rotate(x, cos, sin):
    half = MLA_ROPE // 2
    lane = lax.broadcasted_iota(jnp.int32, x.shape, 1)
    rot = jnp.where(lane < half, -pltpu.roll(x, LANES - half, 1), pltpu.roll(x, half, 1))
    return x * cos + rot * sin


def _ln_in_kernel(xa_ref, xb_ref, g_ref, b_ref, h_ref, hb_ref, *, n_a):
    def emit(x_ref):
        y = _layer_norm(x_ref[...], g_ref[...], b_ref[...])
        h_ref[...] = y
        hb_ref[...] = y.astype(BF16)

    @pl.when(pl.program_id(0) < n_a)
    def _():
        emit(xa_ref)

    @pl.when(pl.program_id(0) >= n_a)
    def _():
        emit(xb_ref)


def _ln_in(xa, xb, g, b):
    ta, d = xa.shape
    tb = xb.shape[0]
    bm = _block(np.gcd(ta, tb), 256)
    n_a, n_b = ta // bm, tb // bm
    row = pl.BlockSpec((1, d), lambda i: (0, 0))
    out = pl.BlockSpec((bm, d), lambda i: (i, 0))
    return pl.pallas_call(
        functools.partial(_ln_in_kernel, n_a=n_a),
        grid=(n_a + n_b,),
        in_specs=[pl.BlockSpec((bm, d), lambda i: (jnp.minimum(i, n_a - 1), 0)),
                  pl.BlockSpec((bm, d), lambda i: (jnp.maximum(i - n_a, 0), 0)),
                  row, row],
        out_specs=[out, out],
        out_shape=[jax.ShapeDtypeStruct((ta + tb, d), F32),
                   jax.ShapeDtypeStruct((ta + tb, d), BF16)],
        compiler_params=_params(1),
        name="ln_in",
    )(xa, xb, g.reshape(1, d), b.reshape(1, d))


def _res_ln_kernel(h_ref, y_ref, g_ref, b_ref, *o_refs):
    z = _layer_norm(DN_ALPHA * h_ref[...] + y_ref[...], g_ref[...], b_ref[...])
    for o_ref in o_refs:
        o_ref[...] = z.astype(o_ref.dtype)


def _res_ln(h, y, g, b, out_dtypes, *, row_off=0, rows=None):
    t, d = h.shape
    rows = t if rows is None else rows
    bm = _block(np.gcd(rows, row_off) if row_off else rows, 256)
    off = row_off // bm
    row = pl.BlockSpec((1, d), lambda i: (0, 0))
    src = pl.BlockSpec((bm, d), lambda i: (i + off, 0))
    dst = pl.BlockSpec((bm, d), lambda i: (i, 0))
    return pl.pallas_call(
        _res_ln_kernel,
        grid=(rows // bm,),
        in_specs=[src, src, row, row],
        out_specs=[dst] * len(out_dtypes),
        out_shape=[jax.ShapeDtypeStruct((rows, d), dt) for dt in out_dtypes],
        compiler_params=_params(1),
        name="res_ln",
    )(h, y, g.reshape(1, d), b.reshape(1, d))


def _matmul_kernel(a_ref, b_ref, o_ref, *scratch, nk, relu2):
    part = jnp.dot(a_ref[...], b_ref[...], preferred_element_type=F32)

    def finish(acc):
        if relu2:
            acc = jnp.square(jnp.maximum(acc, 0.0))
        o_ref[...] = acc.astype(o_ref.dtype)

    if nk == 1:
        finish(part)
        return
    acc_ref, = scratch
    k = pl.program_id(2)

    @pl.when(k == 0)
    def _():
        acc_ref[...] = part

    @pl.when(jnp.logical_and(k > 0, k < nk - 1))
    def _():
        acc_ref[...] += part

    @pl.when(k == nk - 1)
    def _():
        finish(acc_ref[...] + part)


def _matmul(a, b, out_dtype, *, bm=1024, bn=1024, bk=4096, relu2=False, name="matmul"):
    m, kdim = a.shape
    n = b.shape[1]
    bm, bn, bk = _block(m, bm), _block(n, bn), _block(kdim, bk)
    nk = kdim // bk
    return pl.pallas_call(
        functools.partial(_matmul_kernel, nk=nk, relu2=relu2),
        grid=(m // bm, n // bn, nk),
        in_specs=[pl.BlockSpec((bm, bk), lambda i, j, k: (i, k)),
                  pl.BlockSpec((bk, bn), lambda i, j, k: (k, j))],
        out_specs=pl.BlockSpec((bm, bn), lambda i, j, k: (i, j)),
        out_shape=jax.ShapeDtypeStruct((m, n), out_dtype),
        scratch_shapes=[pltpu.VMEM((bm, bn), F32)] if nk > 1 else [],
        compiler_params=_params(3),
        name=name,
    )(a, b)


def _matmul_wres_kernel(a_ref, w_ref, o_ref, wb_ref, *, relu2):
    @pl.when(pl.program_id(1) == 0)
    def _():
        wb_ref[...] = w_ref[...].astype(BF16)

    acc = jnp.dot(a_ref[...], wb_ref[...], preferred_element_type=F32)
    if relu2:
        acc = jnp.square(jnp.maximum(acc, 0.0))
    o_ref[...] = acc.astype(o_ref.dtype)


def _matmul_wres(a, w, out_dtype, *, bm=512, bn=1024, relu2=False, name="matmul_wres"):
    m, kdim = a.shape
    n = w.shape[1]
    bm, bn = _block(m, bm), _block(n, bn)
    return pl.pallas_call(
        functools.partial(_matmul_wres_kernel, relu2=relu2),
        grid=(n // bn, m // bm),
        in_specs=[pl.BlockSpec((bm, kdim), lambda j, i: (i, 0)),
                  pl.BlockSpec((kdim, bn), lambda j, i: (0, j))],
        out_specs=pl.BlockSpec((bm, bn), lambda j, i: (i, j)),
        out_shape=jax.ShapeDtypeStruct((m, n), out_dtype),
        scratch_shapes=[pltpu.VMEM((kdim, bn), BF16)],
        compiler_params=_params(2),
        name=name,
    )(a, w)


def _rope_tables(seq_len):
    half = MLA_ROPE // 2
    inv = 1.0 / (ROPE_THETA ** (jnp.arange(0, MLA_ROPE, 2, dtype=F32) / MLA_ROPE))
    ang = jnp.arange(seq_len, dtype=F32)[:, None] * inv[None, :]
    zero = jnp.zeros((seq_len, LANES - 2 * half), F32)
    cos = jnp.concatenate([jnp.cos(ang), jnp.cos(ang), zero], axis=1)
    sin = jnp.concatenate([jnp.sin(ang), jnp.sin(ang), zero], axis=1)
    return cos, sin


def _pos_block_map(groups, bm):
    (_, _, s_a), (off_b, _, s_b) = groups
    n_a = off_b // bm

    def index(i):
        return jnp.where(i < n_a, i % (s_a // bm), (i - n_a) % (s_b // bm))
    return index


def _qproj_kernel(c_ref, g_ref, w_ref, cos_ref, sin_ref, o_ref, xn_ref, *, heads, scale):
    @pl.when(pl.program_id(1) == 0)
    def _():
        xn_ref[...] = _rms_norm(c_ref[...], g_ref[...]).astype(BF16)

    r = jnp.dot(xn_ref[...], w_ref[...], preferred_element_type=F32)
    cos, sin = cos_ref[...], sin_ref[...]
    for h in range(heads):
        lo = h * MLA_QK_PAD
        o_ref[:, lo:lo + MLA_NOPE] = (r[:, lo:lo + MLA_NOPE] * scale).astype(BF16)
        o_ref[:, lo + MLA_NOPE:lo + MLA_QK_PAD] = (
            _rope(r[:, lo + MLA_NOPE:lo + MLA_QK_PAD], cos, sin) * scale).astype(BF16)


def _q_proj(p, c_off, q_lora, g, wq, cos, sin, groups, bm):
    t = p.shape[0]
    heads = min(4, MLA_HEADS)
    bn = heads * MLA_QK_PAD
    pos = _pos_block_map(groups, bm)
    scale = (MLA_NOPE + MLA_ROPE) ** -0.5 * np.log2(np.e)
    return pl.pallas_call(
        functools.partial(_qproj_kernel, heads=heads, scale=scale),
        grid=(t // bm, MLA_HEADS // heads),
        in_specs=[pl.BlockSpec((bm, q_lora), lambda i, j: (i, c_off // q_lora)),
                  pl.BlockSpec((1, q_lora), lambda i, j: (0, 0)),
                  pl.BlockSpec((q_lora, bn), lambda i, j: (0, j)),
                  pl.BlockSpec((bm, LANES), lambda i, j: (pos(i), 0)),
                  pl.BlockSpec((bm, LANES), lambda i, j: (pos(i), 0))],
        out_specs=pl.BlockSpec((bm, bn), lambda i, j: (i, j)),
        out_shape=jax.ShapeDtypeStruct((t, MLA_HEADS * MLA_QK_PAD), BF16),
        scratch_shapes=[pltpu.VMEM((bm, q_lora), BF16)],
        compiler_params=_params(2),
        name="mla_q_proj",
    )(p, g.reshape(1, q_lora), wq, cos, sin)


def _kvproj_kernel(c_ref, kr_ref, g_ref, wk_ref, wv_ref, cos_ref, sin_ref, k_ref, v_ref):
    xn = _rms_norm(c_ref[...], g_ref[...]).astype(BF16)
    kn = jnp.dot(xn, wk_ref[...], preferred_element_type=F32)
    vn = jnp.dot(xn, wv_ref[...], preferred_element_type=F32)
    k_rope = _rope_rotate(kr_ref[...], cos_ref[...], sin_ref[...]).astype(BF16)
    ones = jnp.ones((k_ref.shape[0], MLA_V_PAD - MLA_V), BF16)
    for h in range(MLA_HEADS):
        lo = h * MLA_QK_PAD
        k_ref[:, lo:lo + MLA_NOPE] = kn[:, h * MLA_NOPE:(h + 1) * MLA_NOPE].astype(BF16)
        k_ref[:, lo + MLA_NOPE:lo + MLA_QK_PAD] = k_rope
        lo = h * MLA_V_PAD
        v_ref[:, lo:lo + MLA_V] = vn[:, h * MLA_V:(h + 1) * MLA_V].astype(BF16)
        v_ref[:, lo + MLA_V:lo + MLA_V_PAD] = ones


def _kv_proj(p, c_off, kv_lora, kr_off, g, wk, wv, cos, sin, groups, bm):
    t = p.shape[0]
    pos = _pos_block_map(groups, bm)
    nk, nv = MLA_HEADS * MLA_QK_PAD, MLA_HEADS * MLA_V
    return pl.pallas_call(
        _kvproj_kernel,
        grid=(t // bm,),
        in_specs=[pl.BlockSpec((bm, kv_lora), lambda i: (i, c_off // kv_lora)),
                  pl.BlockSpec((bm, LANES), lambda i: (i, kr_off // LANES)),
                  pl.BlockSpec((1, kv_lora), lambda i: (0, 0)),
                  pl.BlockSpec((kv_lora, MLA_HEADS * MLA_NOPE), lambda i: (0, 0)),
                  pl.BlockSpec((kv_lora, nv), lambda i: (0, 0)),
                  pl.BlockSpec((bm, LANES), lambda i: (pos(i), 0)),
                  pl.BlockSpec((bm, LANES), lambda i: (pos(i), 0))],
        out_specs=[pl.BlockSpec((bm, nk), lambda i: (i, 0)),
                   pl.BlockSpec((bm, MLA_HEADS * MLA_V_PAD), lambda i: (i, 0))],
        out_shape=[jax.ShapeDtypeStruct((t, nk), BF16),
                   jax.ShapeDtypeStruct((t, MLA_HEADS * MLA_V_PAD), BF16)],
        compiler_params=_params(1),
        name="mla_kv_proj",
    )(p, p, g.reshape(1, kv_lora), wk, wv, cos, sin)


def _group_call(kernel, *, grid, in_specs, out_spec, out, args, scratch_shapes=(), name):
    def body(*refs):
        n_in = len(in_specs)
        kernel(*refs[:n_in], *refs[n_in + 1:])

    return pl.pallas_call(
        body,
        grid=grid,
        in_specs=list(in_specs) + [pl.BlockSpec(memory_space=pl.ANY)],
        out_specs=out_spec,
        out_shape=jax.ShapeDtypeStruct(out.shape, out.dtype),
        input_output_aliases={len(in_specs): 0},
        scratch_shapes=list(scratch_shapes),
        compiler_params=_params(len(grid)),
        name=name,
    )(*args, out)


def _softmax_pv(s, v):
    m = jnp.max(s, -1, keepdims=True)
    e = jnp.exp(s - m)
    l = jnp.sum(e, -1, keepdims=True)
    return jnp.dot(e.astype(BF16), v, preferred_element_type=F32) / l


def _mla_attn_kernel(q_ref, k_ref, v_ref, o_ref, *, chains):
    for c in range(chains):
        rows = pl.ds(c * MLA_Q_ROWS, MLA_Q_ROWS)
        s = lax.dot_general(q_ref[rows, :], k_ref[...], NT_DIMS, preferred_element_type=F32)
        e = jnp.exp2(s - jnp.max(s, -1, keepdims=True)).astype(BF16)
        acc = jnp.dot(e, v_ref[...], preferred_element_type=F32)
        o_ref[rows, :] = (acc[:, :MLA_V] / acc[:, MLA_V:]).astype(o_ref.dtype)


def _mla_attention(q, k, v, groups):
    t = q.shape[0]
    out = jnp.zeros((t, MLA_HEADS * MLA_V), BF16)
    for row_off, batch, seq in groups:
        bq = _block(seq, 8 * MLA_Q_ROWS)
        nq = seq // bq
        q_row = lambda b, h, i, nq=nq, o=row_off // bq: (o + b * nq + i, h)
        kv_row = lambda b, h, i, o=row_off // seq: (o + b, h)
        out = _group_call(
            functools.partial(_mla_attn_kernel, chains=bq // MLA_Q_ROWS),
            grid=(batch, MLA_HEADS, nq),
            in_specs=[pl.BlockSpec((bq, MLA_QK_PAD), q_row),
                      pl.BlockSpec((seq, MLA_QK_PAD), kv_row),
                      pl.BlockSpec((seq, MLA_V_PAD), kv_row)],
            out_spec=pl.BlockSpec((bq, MLA_V), q_row),
            out=out, args=(q, k, v), name="mla_attention")
    return out


GLA_SUPER = 4 * GLA_CHUNK


def _cumsum_rows(tri_b, x):
    hi = x.astype(BF16)
    r1 = x - hi.astype(F32)
    mid = r1.astype(BF16)
    lo = (r1 - mid.astype(F32)).astype(BF16)
    dot = lambda t: jnp.dot(tri_b, t, preferred_element_type=F32)
    return dot(hi) + dot(mid) + dot(lo)


def _gla_kernel(*refs, reverse, is_first, final):
    if final:
        (q_ref, k_ref, v_ref, lr_ref, w2_ref, gb_ref, fwd_ref, gr_ref, ng_ref,
         o_ref, st_ref) = refs
    else:
        q_ref, k_ref, v_ref, lr_ref, w2_ref, gb_ref, o_ref, st_ref = refs
    c, n_rows = GLA_CHUNK, GLA_SUPER
    nb = n_rows // c

    @pl.when(is_first(pl.program_id(0)))
    def _():
        st_ref[...] = jnp.zeros_like(st_ref)

    gate = jnp.dot(lr_ref[...].astype(BF16), w2_ref[...], preferred_element_type=F32) + gb_ref[...]
    log_a_all = ((jnp.minimum(gate, 0.0) - jnp.log1p(jnp.exp(-jnp.abs(gate))))
                 * (1.0 / GLA_GATE_NORM))

    row = lax.broadcasted_iota(jnp.int32, (n_rows, n_rows), 0)
    col = lax.broadcasted_iota(jnp.int32, (n_rows, n_rows), 1)
    before = (col > row) if reverse else (col < row)
    upto = (col >= row) if reverse else (col <= row)
    same_chunk = (row // c) == (col // c)
    same_pair = (row // (2 * c)) == (col // (2 * c))
    m_chunk = jnp.logical_and(same_chunk, upto)
    m_pair = jnp.logical_and(jnp.logical_and(same_pair, jnp.logical_not(same_chunk)), before)
    m_cross = jnp.logical_and(jnp.logical_not(same_pair), before)

    tri_b = upto.astype(BF16)
    order = list(reversed(range(nb))) if reverse else list(range(nb))
    per_chunk = lambda r: jnp.concatenate(
        [jnp.broadcast_to(r[b], (c, GLA_DK)) for b in range(nb)], axis=0)
    nt = lambda a, b: lax.dot_general(a, b, NT_DIMS, preferred_element_type=F32)

    for h in range(GLA_HEADS):
        dk = slice(h * GLA_DK, (h + 1) * GLA_DK)
        dv = slice(h * GLA_DV, (h + 1) * GLA_DV)
        cum = _cumsum_rows(tri_b, log_a_all[:, dk])
        r_start, r_end = {}, {}
        r_prev = jnp.zeros((1, GLA_DK), F32)
        for b in order:
            e = b * c if reverse else b * c + c - 1
            r_start[b] = r_prev
            r_end[b] = cum[e:e + 1]
            r_prev = r_end[b]
        total = r_prev
        mid = r_end[order[nb // 2 - 1]]
        cumloc = cum - per_chunk(r_start)
        to_end = per_chunk(r_end) - cum

        q = q_ref[:, dk] * (GLA_DK ** -0.5)
        k = k_ref[:, dk]
        v = v_ref[:, dv].astype(BF16)
        q_loc = (q * jnp.exp(cumloc)).astype(BF16)
        k_loc = (k * jnp.exp(-cumloc)).astype(BF16)
        k_end = (k * jnp.exp(to_end)).astype(BF16)
        q_mid = (q * jnp.exp(jnp.minimum(cum - mid, 0.0))).astype(BF16)
        k_mid = (k * jnp.exp(jnp.minimum(mid - cum, 0.0))).astype(BF16)
        q_all = (q * jnp.exp(cum)).astype(BF16)
        k_all = (k * jnp.exp(total - cum)).astype(BF16)

        attn = jnp.where(m_chunk, nt(q_loc, k_loc),
                         jnp.where(m_pair, nt(q_loc, k_end),
                                   jnp.where(m_cross, nt(q_mid, k_mid), 0.0))).astype(BF16)
        state = st_ref[h]
        o = jnp.dot(attn, v, preferred_element_type=F32) + nt(q_all, state.astype(BF16))
        st_ref[h] = state * jnp.exp(total) + lax.dot_general(
            v, k_all, TN_DIMS, preferred_element_type=F32)
        if final:
            tot = fwd_ref[:, dv] + o
            gr = gr_ref[:, dv]
            o_ref[:, dv] = (_rms_norm(tot, ng_ref[...]) * (gr * _sigmoid(gr))).astype(o_ref.dtype)
        else:
            o_ref[:, dv] = o


def _gla_direction(p, p_glr, offs, w2cat, gbcat, groups, *, reverse, fwd=None, norm_g=None):
    t = p.shape[0]
    final = fwd is not None
    lb = GLA_SUPER
    (_, _, seq_a), (off_b, _, seq_b) = groups
    n_a, ns_a, ns_b = off_b // lb, seq_a // lb, seq_b // lb
    assert seq_a % lb == 0 and seq_b % lb == 0
    d = 1 if reverse else 0

    def local(u):
        in_a = u < n_a
        return jnp.where(in_a, u % ns_a, (u - n_a) % ns_b), jnp.where(in_a, ns_a, ns_b)

    def rows(u):
        n, ns = local(u)
        return (u - n) + (ns - 1 - n) if reverse else u

    is_first = lambda u: local(u)[0] == 0
    key, val = GLA_HEADS * GLA_DK, GLA_HEADS * GLA_DV
    assert offs["gq"] % key == 0 and offs["gk"] % key == 0
    assert offs["gv"] % val == 0 and offs["gr"] % val == 0
    col = lambda off, width: (lambda u: (rows(u), off // width))
    in_specs = [pl.BlockSpec((lb, key), col(offs["gq"], key)),
                pl.BlockSpec((lb, key), col(offs["gk"], key)),
                pl.BlockSpec((lb, val), col(offs["gv"], val)),
                pl.BlockSpec((lb, LANES), col(0, LANES)),
                pl.BlockSpec((LANES, key), lambda u: (0, d)),
                pl.BlockSpec((1, key), lambda u: (0, d))]
    args = [p, p, p, p_glr, w2cat, gbcat]
    if final:
        in_specs += [pl.BlockSpec((lb, val), col(0, val)),
                     pl.BlockSpec((lb, val), col(offs["gr"], val)),
                     pl.BlockSpec((1, GLA_DV), lambda u: (0, 0))]
        args += [fwd, p, norm_g.reshape(1, GLA_DV)]
    return pl.pallas_call(
        functools.partial(_gla_kernel, reverse=reverse, is_first=is_first, final=final),
        grid=(t // lb,),
        in_specs=in_specs,
        out_specs=pl.BlockSpec((lb, val), col(0, val)),
        out_shape=jax.ShapeDtypeStruct((t, val), BF16 if final else F32),
        scratch_shapes=[pltpu.VMEM((GLA_HEADS, GLA_DV, GLA_DK), F32)],
        compiler_params=_params(1),
        name="gla_bwd" if reverse else "gla_fwd",
    )(*args)


def _merge_kernel(a1_ref, a2_ref, w1_ref, w2_ref, g1_ref, g2_ref, b1_ref, b2_ref, o_ref,
                  w1b_ref, w2b_ref):
    @pl.when(pl.program_id(1) == 0)
    def _():
        w1b_ref[...] = w1_ref[...].astype(BF16)
        w2b_ref[...] = w2_ref[...].astype(BF16)

    y1 = jnp.dot(a1_ref[...], w1b_ref[...], preferred_element_type=F32)
    y2 = jnp.dot(a2_ref[...], w2b_ref[...], preferred_element_type=F32)
    s1 = _sigmoid(g1_ref[...] + b1_ref[...])
    s2 = _sigmoid(g2_ref[...] + b2_ref[...])
    o_ref[...] = (s1 * y1 + s2 * y2).astype(o_ref.dtype)


def _branch_merge(o_mla, o_gla, w_mla, w_gla, p, gm_off, b_merge):
    t, k1 = o_mla.shape
    k2 = o_gla.shape[1]
    d = w_mla.shape[1]
    bm, bn = _block(t, 1024), _block(d, 512)
    g0, nd = gm_off // bn, d // bn
    return pl.pallas_call(
        _merge_kernel,
        grid=(nd, t // bm),
        in_specs=[pl.BlockSpec((bm, k1), lambda j, i: (i, 0)),
                  pl.BlockSpec((bm, k2), lambda j, i: (i, 0)),
                  pl.BlockSpec((k1, bn), lambda j, i: (0, j)),
                  pl.BlockSpec((k2, bn), lambda j, i: (0, j)),
                  pl.BlockSpec((bm, bn), lambda j, i: (i, g0 + j)),
                  pl.BlockSpec((bm, bn), lambda j, i: (i, g0 + nd + j)),
                  pl.BlockSpec((1, bn), lambda j, i: (0, j)),
                  pl.BlockSpec((1, bn), lambda j, i: (0, nd + j))],
        out_specs=pl.BlockSpec((bm, bn), lambda j, i: (i, j)),
        out_shape=jax.ShapeDtypeStruct((t, d), BF16),
        scratch_shapes=[pltpu.VMEM((k1, bn), BF16), pltpu.VMEM((k2, bn), BF16)],
        compiler_params=_params(2),
        name="branch_merge",
    )(o_mla, o_gla, w_mla, w_gla, p, p, b_merge.reshape(1, 2 * d), b_merge.reshape(1, 2 * d))


def _xattn_kernel(q_ref, k_ref, v_ref, o_ref, *, scale):
    s = lax.dot_general(q_ref[...], k_ref[...], NT_DIMS, preferred_element_type=F32) * scale
    o_ref[...] = _softmax_pv(s, v_ref[...]).astype(o_ref.dtype)


def _cross_attention(q, kv, groups, mem_groups, n_mem):
    t, d = q.shape
    hd = d // XA_HEADS
    out = jnp.zeros((t, d), BF16)
    for (row_off, batch, seq), mem_off in zip(groups, mem_groups):
        bq = _block(seq, 512)
        nq = seq // bq
        q_row = lambda b, i, h, nq=nq, o=row_off // bq: (o + b * nq + i, h)
        out = _group_call(
            functools.partial(_xattn_kernel, scale=hd ** -0.5),
            grid=(batch, nq, XA_HEADS),
            in_specs=[pl.BlockSpec((bq, hd), q_row),
                      pl.BlockSpec((n_mem, hd), lambda b, i, h, o=mem_off // n_mem: (o + b, h)),
                      pl.BlockSpec((n_mem, hd),
                                   lambda b, i, h, o=mem_off // n_mem: (o + b, XA_HEADS + h))],
            out_spec=pl.BlockSpec((bq, hd), q_row),
            out=out, args=(q, kv, kv), name="cross_attention")
    return out


def _in_proj_layout(d_model, q_lora, kv_lora):
    gla_key, gla_val = GLA_HEADS * GLA_DK, GLA_HEADS * GLA_DV
    b_width = 2 * gla_key + 2 * gla_val
    s_kr = q_lora + kv_lora
    s_gq = s_kr + MLA_ROPE
    s_glr = s_gq + b_width
    s_gm = s_glr + 2 * GLA_GATE_RANK
    pb = 1024 if b_width % 1024 == 0 and (2 * d_model) % 1024 == 0 else 512
    assert b_width % pb == 0 and (2 * d_model) % pb == 0
    assert s_kr % LANES == 0 and q_lora % kv_lora == 0 and 2 * GLA_GATE_RANK <= LANES
    assert s_gq % 8 == 0 and s_glr % 8 == 0 and s_gm % 8 == 0
    n_a = -(-s_gq // pb)
    gq = n_a * pb
    starts = ([pb * j for j in range(n_a)] + [s_gq + pb * j for j in range(b_width // pb)]
              + [s_gm + pb * j for j in range(2 * d_model // pb)])
    offs = {"cq": 0, "ckv": q_lora, "krope": s_kr, "gq": gq, "gk": gq + gla_key,
            "gv": gq + 2 * gla_key, "gr": gq + 2 * gla_key + gla_val, "gm": gq + b_width}
    return offs, starts, pb, s_glr


def _in_proj_kernel(a_ref, wt_ref, o_ref, wb_ref):
    @pl.when(pl.program_id(1) == 0)
    def _():
        wb_ref[...] = wt_ref[...].astype(BF16)

    o_ref[...] = lax.dot_general(a_ref[...], wb_ref[...], NT_DIMS, preferred_element_type=F32)


def _in_proj(a, w_t, starts, bn, *, bm, name):
    m, kdim = a.shape
    bm = _block(m, bm)

    sub = 8
    assert all(s % sub == 0 for s in starts)

    def row_start(j):
        r = jnp.int32(starts[-1] // sub)
        for idx in range(len(starts) - 2, -1, -1):
            r = jnp.where(j == idx, jnp.int32(starts[idx] // sub), r)
        return pl.multiple_of(r * sub, sub)

    return pl.pallas_call(
        _in_proj_kernel,
        grid=(len(starts), m // bm),
        in_specs=[pl.BlockSpec((bm, kdim), lambda j, i: (i, 0)),
                  pl.BlockSpec((pl.Element(bn), pl.Element(kdim)),
                               lambda j, i: (row_start(j), 0))],
        out_specs=pl.BlockSpec((bm, bn), lambda j, i: (i, j)),
        out_shape=jax.ShapeDtypeStruct((m, bn * len(starts)), F32),
        scratch_shapes=[pltpu.VMEM((bn, kdim), BF16)],
        compiler_params=_params(2),
        name=name,
    )(a, w_t)


def _pack_uq(w_uq):
    q_lora = w_uq.shape[0]
    w = w_uq.reshape(q_lora, MLA_HEADS, MLA_NOPE + MLA_ROPE)
    rope = w[:, :, MLA_NOPE:]
    half = MLA_ROPE // 2
    rot = jnp.concatenate([-rope[:, :, half:], rope[:, :, :half]], axis=2)
    return jnp.concatenate([w, rot], axis=2).reshape(q_lora, MLA_HEADS * MLA_QK_PAD).astype(BF16)


def _pack_ukv(w_ukv):
    kv_lora = w_ukv.shape[0]
    w = w_ukv.reshape(kv_lora, MLA_HEADS, MLA_NOPE + MLA_V)
    wk = w[:, :, :MLA_NOPE].reshape(kv_lora, MLA_HEADS * MLA_NOPE)
    wv = w[:, :, MLA_NOPE:].reshape(kv_lora, MLA_HEADS * MLA_V)
    return wk.astype(BF16), wv.astype(BF16)


def _pack_gate(w2, gb):
    r, key = w2.shape[1], w2.shape[2]
    z = jnp.zeros((r, key), w2.dtype)
    top = jnp.concatenate([w2[0], z], axis=1)
    bot = jnp.concatenate([z, w2[1]], axis=1)
    tail = jnp.zeros((LANES - 2 * r, 2 * key), w2.dtype)
    return jnp.concatenate([top, bot, tail], axis=0).astype(BF16), gb.reshape(1, 2 * key)


def kernel(x_prompt, x_sample, mem_prompt, mem_sample, ln_in_g, ln_in_b, w_in, b_merge, mla_q_norm, w_uq, mla_kv_norm, w_ukv, gla_gate_w2, gla_gate_b, gla_norm, w_branch_mla, w_branch_gla, w_mix_out, ln1_g, ln1_b, xa_wq, xa_wkv, xa_wo, ln2_g, ln2_b, mlp_w1, mlp_w2, ln3_g, ln3_b):
    assert w_in.shape[0] == DEPTH
    ba, sa, d = x_prompt.shape
    bb, sb, _ = x_sample.shape
    n_mem = mem_prompt.shape[1]
    ta, tb = ba * sa, bb * sb
    groups = ((0, ba, sa), (ta, bb, sb))
    mem_groups = (0, ba * n_mem)
    q_lora, kv_lora = mla_q_norm.shape[1], mla_kv_norm.shape[1]
    assert ta % sb == 0 and ta % 1024 == 0 and sa % 1024 == 0 and sb % 1024 == 0

    offs, in_starts, in_bn, glr_start = _in_proj_layout(d, q_lora, kv_lora)
    w_in_t = jnp.swapaxes(w_in[0], 0, 1)
    wq_p = _pack_uq(w_uq[0])
    wk_p, wv_p = _pack_ukv(w_ukv[0])
    w2cat, gbcat = _pack_gate(gla_gate_w2[0], gla_gate_b[0])
    cos, sin = _rope_tables(max(sa, sb))

    h, h_b = _ln_in(x_prompt.reshape(ta, d), x_sample.reshape(tb, d), ln_in_g, ln_in_b)

    p = _in_proj(h_b, w_in_t, in_starts, in_bn, bm=512, name="in_proj")
    p_glr = _in_proj(h_b, w_in_t, [glr_start], LANES, bm=2048, name="in_proj_gate")
    q = _q_proj(p, offs["cq"], q_lora, mla_q_norm[0], wq_p, cos, sin, groups, 1024)
    k, v = _kv_proj(p, offs["ckv"], kv_lora, offs["krope"], mla_kv_norm[0], wk_p, wv_p,
                    cos, sin, groups, 512)
    o_mla = _mla_attention(q, k, v, groups)
    gla_f = _gla_direction(p, p_glr, offs, w2cat, gbcat, groups, reverse=False)
    o_gla = _gla_direction(p, p_glr, offs, w2cat, gbcat, groups, reverse=True, fwd=gla_f,
                           norm_g=gla_norm[0])
    merged = _branch_merge(o_mla, o_gla, w_branch_mla[0], w_branch_gla[0], p, offs["gm"],
                           b_merge[0])
    mix = _matmul_wres(merged, w_mix_out[0], F32, name="mix_out")
    h, h_b = _res_ln(h, mix, ln1_g[0], ln1_b[0], (F32, BF16))

    mem = jnp.concatenate([mem_prompt.reshape(ba * n_mem, d), mem_sample.reshape(bb * n_mem, d)])
    xq = _matmul_wres(h_b, xa_wq[0], BF16, name="xa_q")
    xkv = _matmul_wres(mem.astype(BF16), xa_wkv[0], BF16, bm=2048, bn=512, name="xa_kv")
    o_x = _cross_attention(xq, xkv, groups, mem_groups, n_mem)
    xo = _matmul_wres(o_x, xa_wo[0], F32, name="xa_o")
    h, h_b = _res_ln(h, xo, ln2_g[0], ln2_b[0], (F32, BF16))

    u = _matmul_wres(h_b, mlp_w1[0], BF16, relu2=True, name="mlp_up")
    ff = _matmul(u, mlp_w2[0].astype(BF16), F32, name="mlp_down")
    y_a, = _res_ln(h, ff, ln3_g[0], ln3_b[0], (F32,), row_off=0, rows=ta)
    y_b, = _res_ln(h, ff, ln3_g[0], ln3_b[0], (F32,), row_off=ta, rows=tb)
    return y_a.reshape(ba, sa, d), y_b.reshape(bb, sb, d)
```

```python
import functools

import numpy as np
import jax
import jax.numpy as jnp
from jax import lax
from jax.experimental import pallas as pl
from jax.experimental.pallas import tpu as pltpu

MLA_HEADS = 16
MLA_NOPE = 128
MLA_ROPE = 64
MLA_V = 128
ROPE_THETA = 10000.0
GLA_HEADS = 4
GLA_DK = 256
GLA_DV = 512
GLA_GATE_RANK = 16
GLA_GATE_NORM = 16.0
GLA_CHUNK = 64
XA_HEADS = 4
LN_EPS = 1e-5
RMS_EPS = 1e-6
DEPTH = 1
DN_ALPHA = (2.0 * DEPTH) ** 0.25

LANES = 128
MLA_QK_PAD = 2 * LANES
MLA_V_PAD = 2 * LANES
MLA_Q_ROWS = 256
VMEM_LIMIT_BYTES = 56 * 2**20
MATMUL_LN_VMEM_BYTES = 63 * 2**20

BF16 = jnp.bfloat16
F32 = jnp.float32
NT_DIMS = (((1,), (1,)), ((), ()))
TN_DIMS = (((0,), (0,)), ((), ()))


def _params(n_grid):
    return pltpu.CompilerParams(dimension_semantics=("arbitrary",) * n_grid,
                                vmem_limit_bytes=VMEM_LIMIT_BYTES)


def _block(n, pref):
    b = min(n, pref)
    while n % b:
        b //= 2
    return b


def _round_up(n, m):
    return -(-n // m) * m


def _layer_norm(x, g, b):
    mu = jnp.mean(x, -1, keepdims=True)
    xc = x - mu
    var = jnp.mean(xc * xc, -1, keepdims=True)
    return xc * lax.rsqrt(var + LN_EPS) * g + b


def _rms_norm(x, g):
    return x * lax.rsqrt(jnp.mean(x * x, -1, keepdims=True) + RMS_EPS) * g


def _sigmoid(x):
    return 1.0 / (1.0 + jnp.exp(-x))


def _rope(x, cos, sin):
    return x * cos + pltpu.roll(x, MLA_ROPE, 1) * sin


def _rope_rotate(x, cos, sin):
    half = MLA_ROPE // 2
    lane = lax.broadcasted_iota(jnp.int32, x.shape, 1)
    rot = jnp.where(lane < half, -pltpu.roll(x, LANES - half, 1), pltpu.roll(x, half, 1))
    return x * cos + rot * sin


def _ln_in_kernel(xa_ref, xb_ref, g_ref, b_ref, h_ref, hb_ref, *, n_a):
    def emit(x_ref):
        y = _layer_norm(x_ref[...], g_ref[...], b_ref[...])
        h_ref[...] = y
        hb_ref[...] = y.astype(BF16)

    @pl.when(pl.program_id(0) < n_a)
    def _():
        emit(xa_ref)

    @pl.when(pl.program_id(0) >= n_a)
    def _():
        emit(xb_ref)


def _ln_in(xa, xb, g, b):
    ta, d = xa.shape
    tb = xb.shape[0]
    bm = _block(np.gcd(ta, tb), 256)
    n_a, n_b = ta // bm, tb // bm
    row = pl.BlockSpec((1, d), lambda i: (0, 0))
    out = pl.BlockSpec((bm, d), lambda i: (i, 0))
    return pl.pallas_call(
        functools.partial(_ln_in_kernel, n_a=n_a),
        grid=(n_a + n_b,),
        in_specs=[pl.BlockSpec((bm, d), lambda i: (jnp.minimum(i, n_a - 1), 0)),
                  pl.BlockSpec((bm, d), lambda i: (jnp.maximum(i - n_a, 0), 0)),
                  row, row],
        out_specs=[out, out],
        out_shape=[jax.ShapeDtypeStruct((ta + tb, d), F32),
                   jax.ShapeDtypeStruct((ta + tb, d), BF16)],
        compiler_params=_params(1),
        name="ln_in",
    )(xa, xb, g.reshape(1, d), b.reshape(1, d))


def _res_ln_kernel(h_ref, y_ref, g_ref, b_ref, *o_refs):
    z = _layer_norm(DN_ALPHA * h_ref[...] + y_ref[...], g_ref[...], b_ref[...])
    for o_ref in o_refs:
        o_ref[...] = z.astype(o_ref.dtype)


def _res_ln(h, y, g, b, out_dtypes, *, row_off=0, rows=None):
    t, d = h.shape
    rows = t if rows is None else rows
    bm = _block(np.gcd(rows, row_off) if row_off else rows, 256)
    off = row_off // bm
    row = pl.BlockSpec((1, d), lambda i: (0, 0))
    src = pl.BlockSpec((bm, d), lambda i: (i + off, 0))
    dst = pl.BlockSpec((bm, d), lambda i: (i, 0))
    return pl.pallas_call(
        _res_ln_kernel,
        grid=(rows // bm,),
        in_specs=[src, src, row, row],
        out_specs=[dst] * len(out_dtypes),
        out_shape=[jax.ShapeDtypeStruct((rows, d), dt) for dt in out_dtypes],
        compiler_params=_params(1),
        name="res_ln",
    )(h, y, g.reshape(1, d), b.reshape(1, d))


def _matmul_kernel(a_ref, b_ref, o_ref, *scratch, nk, relu2):
    part = jnp.dot(a_ref[...], b_ref[...], preferred_element_type=F32)

    def finish(acc):
        if relu2:
            acc = jnp.square(jnp.maximum(acc, 0.0))
        o_ref[...] = acc.astype(o_ref.dtype)

    if nk == 1:
        finish(part)
        return
    acc_ref, = scratch
    k = pl.program_id(2)

    @pl.when(k == 0)
    def _():
        acc_ref[...] = part

    @pl.when(jnp.logical_and(k > 0, k < nk - 1))
    def _():
        acc_ref[...] += part

    @pl.when(k == nk - 1)
    def _():
        finish(acc_ref[...] + part)


def _matmul(a, b, out_dtype, *, bm=1024, bn=1024, bk=4096, relu2=False, name="matmul"):
    m, kdim = a.shape
    n = b.shape[1]
    bm, bn, bk = _block(m, bm), _block(n, bn), _block(kdim, bk)
    nk = kdim // bk
    return pl.pallas_call(
        functools.partial(_matmul_kernel, nk=nk, relu2=relu2),
        grid=(m // bm, n // bn, nk),
        in_specs=[pl.BlockSpec((bm, bk), lambda i, j, k: (i, k)),
                  pl.BlockSpec((bk, bn), lambda i, j, k: (k, j))],
        out_specs=pl.BlockSpec((bm, bn), lambda i, j, k: (i, j)),
        out_shape=jax.ShapeDtypeStruct((m, n), out_dtype),
        scratch_shapes=[pltpu.VMEM((bm, bn), F32)] if nk > 1 else [],
        compiler_params=_params(3),
        name=name,
    )(a, b)


def _matmul_wres_kernel(a_ref, w_ref, o_ref, wb_ref, *, relu2):
    @pl.when(pl.program_id(1) == 0)
    def _():
        wb_ref[...] = w_ref[...].astype(BF16)

    acc = jnp.dot(a_ref[...], wb_ref[...], preferred_element_type=F32)
    if relu2:
        acc = jnp.square(jnp.maximum(acc, 0.0))
    o_ref[...] = acc.astype(o_ref.dtype)


def _matmul_wres(a, w, out_dtype, *, bm=512, bn=1024, relu2=False, name="matmul_wres"):
    m, kdim = a.shape
    n = w.shape[1]
    bm, bn = _block(m, bm), _block(n, bn)
    return pl.pallas_call(
        functools.partial(_matmul_wres_kernel, relu2=relu2),
        grid=(n // bn, m // bm),
        in_specs=[pl.BlockSpec((bm, kdim), lambda j, i: (i, 0)),
                  pl.BlockSpec((kdim, bn), lambda j, i: (0, j))],
        out_specs=pl.BlockSpec((bm, bn), lambda j, i: (i, j)),
        out_shape=jax.ShapeDtypeStruct((m, n), out_dtype),
        scratch_shapes=[pltpu.VMEM((kdim, bn), BF16)],
        compiler_params=_params(2),
        name=name,
    )(a, w)


def _matmul_ln_kernel(a_ref, w_ref, h_ref, g_ref, b_ref, o_ref, ob_ref, *, chains):
    rows_per = a_ref.shape[0] // chains
    for c in range(chains):
        rows = pl.ds(c * rows_per, rows_per)
        y = jnp.dot(a_ref[rows, :], w_ref[...], preferred_element_type=F32)
        z = _layer_norm(DN_ALPHA * h_ref[rows, :] + y, g_ref[...], b_ref[...])
        o_ref[rows, :] = z
        ob_ref[rows, :] = z.astype(BF16)


def _matmul_ln(a, w, h, g, b, *, bm, name):
    m, kdim = a.shape
    n = w.shape[1]
    bm = _block(m, bm)
    row = pl.BlockSpec((1, n), lambda i: (0, 0))
    tile = pl.BlockSpec((bm, n), lambda i: (i, 0))
    return pl.pallas_call(
        functools.partial(_matmul_ln_kernel, chains=2 if bm % 256 == 0 else 1),
        grid=(m // bm,),
        in_specs=[pl.BlockSpec((bm, kdim), lambda i: (i, 0)),
                  pl.BlockSpec((kdim, n), lambda i: (0, 0), pipeline_mode=pl.Buffered(1)),
                  tile, row, row],
        out_specs=[tile, tile],
        out_shape=[jax.ShapeDtypeStruct((m, n), F32), jax.ShapeDtypeStruct((m, n), BF16)],
        compiler_params=pltpu.CompilerParams(dimension_semantics=("arbitrary",),
                                             vmem_limit_bytes=MATMUL_LN_VMEM_BYTES),
        name=name,
    )(a, w, h, g.reshape(1, n), b.reshape(1, n))


def _rope_tables(seq_len):
    half = MLA_ROPE // 2
    inv = 1.0 / (ROPE_THETA ** (jnp.arange(0, MLA_ROPE, 2, dtype=F32) / MLA_ROPE))
    ang = jnp.arange(seq_len, dtype=F32)[:, None] * inv[None, :]
    zero = jnp.zeros((seq_len, LANES - 2 * half), F32)
    cos = jnp.concatenate([jnp.cos(ang), jnp.cos(ang), zero], axis=1)
    sin = jnp.concatenate([jnp.sin(ang), jnp.sin(ang), zero], axis=1)
    return cos, sin


def _pos_block_map(groups, bm):
    (_, _, s_a), (off_b, _, s_b) = groups
    n_a = off_b // bm

    def index(i):
        return jnp.where(i < n_a, i % (s_a // bm), (i - n_a) % (s_b // bm))
    return index


def _qproj_kernel(c_ref, g_ref, w_ref, cos_ref, sin_ref, o_ref, xn_ref, *, heads, scale):
    @pl.when(pl.program_id(1) == 0)
    def _():
        xn_ref[...] = _rms_norm(c_ref[...], g_ref[...]).astype(BF16)

    r = jnp.dot(xn_ref[...], w_ref[...], preferred_element_type=F32)
    cos, sin = cos_ref[...], sin_ref[...]
    for h in range(heads):
        lo = h * MLA_QK_PAD
        o_ref[:, lo:lo + MLA_NOPE] = (r[:, lo:lo + MLA_NOPE] * scale).astype(BF16)
        o_ref[:, lo + MLA_NOPE:lo + MLA_QK_PAD] = (
            _rope(r[:, lo + MLA_NOPE:lo + MLA_QK_PAD], cos, sin) * scale).astype(BF16)


def _q_proj(p, c_off, q_lora, g, wq, cos, sin, groups, bm):
    t = p.shape[0]
    heads = min(4, MLA_HEADS)
    bn = heads * MLA_QK_PAD
    pos = _pos_block_map(groups, bm)
    scale = (MLA_NOPE + MLA_ROPE) ** -0.5 * np.log2(np.e)
    return pl.pallas_call(
        functools.partial(_qproj_kernel, heads=heads, scale=scale),
        grid=(t // bm, MLA_HEADS // heads),
        in_specs=[pl.BlockSpec((bm, q_lora), lambda i, j: (i, c_off // q_lora)),
                  pl.BlockSpec((1, q_lora), lambda i, j: (0, 0)),
                  pl.BlockSpec((q_lora, bn), lambda i, j: (0, j)),
                  pl.BlockSpec((bm, LANES), lambda i, j: (pos(i), 0)),
                  pl.BlockSpec((bm, LANES), lambda i, j: (pos(i), 0))],
        out_specs=pl.BlockSpec((bm, bn), lambda i, j: (i, j)),
        out_shape=jax.ShapeDtypeStruct((t, MLA_HEADS * MLA_QK_PAD), BF16),
        scratch_shapes=[pltpu.VMEM((bm, q_lora), BF16)],
        compiler_params=_params(2),
        name="mla_q_proj",
    )(p, g.reshape(1, q_lora), wq, cos, sin)


def _kvproj_kernel(c_ref, kr_ref, g_ref, wk_ref, wv_ref, cos_ref, sin_ref, k_ref, v_ref):
    xn = _rms_norm(c_ref[...], g_ref[...]).astype(BF16)
    kn = jnp.dot(xn, wk_ref[...], preferred_element_type=F32)
    vn = jnp.dot(xn, wv_ref[...], preferred_element_type=F32)
    k_rope = _rope_rotate(kr_ref[...], cos_ref[...], sin_ref[...]).astype(BF16)
    ones = jnp.ones((k_ref.shape[0], MLA_V_PAD - MLA_V), BF16)
    for h in range(MLA_HEADS):
        lo = h * MLA_QK_PAD
        k_ref[:, lo:lo + MLA_NOPE] = kn[:, h * MLA_NOPE:(h + 1) * MLA_NOPE].astype(BF16)
        k_ref[:, lo + MLA_NOPE:lo + MLA_QK_PAD] = k_rope
        lo = h * MLA_V_PAD
        v_ref[:, lo:lo + MLA_V] = vn[:, h * MLA_V:(h + 1) * MLA_V].astype(BF16)
        v_ref[:, lo + MLA_V:lo + MLA_V_PAD] = ones


def _kv_proj(p, c_off, kv_lora, kr_off, g, wk, wv, cos, sin, groups, bm):
    t = p.shape[0]
    pos = _pos_block_map(groups, bm)
    nk, nv = MLA_HEADS * MLA_QK_PAD, MLA_HEADS * MLA_V
    return pl.pallas_call(
        _kvproj_kernel,
        grid=(t // bm,),
        in_specs=[pl.BlockSpec((bm, kv_lora), lambda i: (i, c_off // kv_lora)),
                  pl.BlockSpec((bm, LANES), lambda i: (i, kr_off // LANES)),
                  pl.BlockSpec((1, kv_lora), lambda i: (0, 0)),
                  pl.BlockSpec((kv_lora, MLA_HEADS * MLA_NOPE), lambda i: (0, 0)),
                  pl.BlockSpec((kv_lora, nv), lambda i: (0, 0)),
                  pl.BlockSpec((bm, LANES), lambda i: (pos(i), 0)),
                  pl.BlockSpec((bm, LANES), lambda i: (pos(i), 0))],
        out_specs=[pl.BlockSpec((bm, nk), lambda i: (i, 0)),
                   pl.BlockSpec((bm, MLA_HEADS * MLA_V_PAD), lambda i: (i, 0))],
        out_shape=[jax.ShapeDtypeStruct((t, nk), BF16),
                   jax.ShapeDtypeStruct((t, MLA_HEADS * MLA_V_PAD), BF16)],
        compiler_params=_params(1),
        name="mla_kv_proj",
    )(p, p, g.reshape(1, kv_lora), wk, wv, cos, sin)


def _group_call(kernel, *, grid, in_specs, out_spec, out, args, scratch_shapes=(), name):
    def body(*refs):
        n_in = len(in_specs)
        kernel(*refs[:n_in], *refs[n_in + 1:])

    return pl.pallas_call(
        body,
        grid=grid,
        in_specs=list(in_specs) + [pl.BlockSpec(memory_space=pl.ANY)],
        out_specs=out_spec,
        out_shape=jax.ShapeDtypeStruct(out.shape, out.dtype),
        input_output_aliases={len(in_specs): 0},
        scratch_shapes=list(scratch_shapes),
        compiler_params=_params(len(grid)),
        name=name,
    )(*args, out)


def _softmax_pv(s, v):
    m = jnp.max(s, -1, keepdims=True)
    e = jnp.exp(s - m)
    l = jnp.sum(e, -1, keepdims=True)
    return jnp.dot(e.astype(BF16), v, preferred_element_type=F32) / l


def _mla_attn_kernel(q_ref, k_ref, v_ref, o_ref, *, chains):
    for c in range(chains):
        rows = pl.ds(c * MLA_Q_ROWS, MLA_Q_ROWS)
        s = lax.dot_general(q_ref[rows, :], k_ref[...], NT_DIMS, preferred_element_type=F32)
        e = jnp.exp2(s - jnp.max(s, -1, keepdims=True)).astype(BF16)
        acc = jnp.dot(e, v_ref[...], preferred_element_type=F32)
        o_ref[rows, :] = (acc[:, :MLA_V] / acc[:, MLA_V:]).astype(o_ref.dtype)


def _mla_attention(q, k, v, groups):
    t = q.shape[0]
    out = jnp.zeros((t, MLA_HEADS * MLA_V), BF16)
    for row_off, batch, seq in groups:
        bq = _block(seq, 8 * MLA_Q_ROWS)
        nq = seq // bq
        q_row = lambda b, h, i, nq=nq, o=row_off // bq: (o + b * nq + i, h)
        kv_row = lambda b, h, i, o=row_off // seq: (o + b, h)
        out = _group_call(
            functools.partial(_mla_attn_kernel, chains=bq // MLA_Q_ROWS),
            grid=(batch, MLA_HEADS, nq),
            in_specs=[pl.BlockSpec((bq, MLA_QK_PAD), q_row),
                      pl.BlockSpec((seq, MLA_QK_PAD), kv_row),
                      pl.BlockSpec((seq, MLA_V_PAD), kv_row)],
            out_spec=pl.BlockSpec((bq, MLA_V), q_row),
            out=out, args=(q, k, v), name="mla_attention")
    return out


GLA_SUPER = 4 * GLA_CHUNK


def _cumsum_rows(tri_b, x):
    hi = x.astype(BF16)
    r1 = x - hi.astype(F32)
    mid = r1.astype(BF16)
    lo = (r1 - mid.astype(F32)).astype(BF16)
    dot = lambda t: jnp.dot(tri_b, t, preferred_element_type=F32)
    return dot(hi) + dot(mid) + dot(lo)


def _gla_kernel(*refs, reverse, is_first, final):
    if final:
        (q_ref, k_ref, v_ref, lr_ref, w2_ref, gb_ref, fwd_ref, gr_ref, ng_ref,
         o_ref, st_ref) = refs
    else:
        q_ref, k_ref, v_ref, lr_ref, w2_ref, gb_ref, o_ref, st_ref = refs
    c, n_rows = GLA_CHUNK, GLA_SUPER
    nb = n_rows // c

    @pl.when(is_first(pl.program_id(0)))
    def _():
        st_ref[...] = jnp.zeros_like(st_ref)

    gate = jnp.dot(lr_ref[...].astype(BF16), w2_ref[...], preferred_element_type=F32) + gb_ref[...]
    log_a_all = ((jnp.minimum(gate, 0.0) - jnp.log1p(jnp.exp(-jnp.abs(gate))))
                 * (1.0 / GLA_GATE_NORM))

    row = lax.broadcasted_iota(jnp.int32, (n_rows, n_rows), 0)
    col = lax.broadcasted_iota(jnp.int32, (n_rows, n_rows), 1)
    before = (col > row) if reverse else (col < row)
    upto = (col >= row) if reverse else (col <= row)
    same_chunk = (row // c) == (col // c)
    same_pair = (row // (2 * c)) == (col // (2 * c))
    m_chunk = jnp.logical_and(same_chunk, upto)
    m_pair = jnp.logical_and(jnp.logical_and(same_pair, jnp.logical_not(same_chunk)), before)
    m_cross = jnp.logical_and(jnp.logical_not(same_pair), before)

    tri_b = upto.astype(BF16)
    order = list(reversed(range(nb))) if reverse else list(range(nb))
    per_chunk = lambda r: jnp.concatenate(
        [jnp.broadcast_to(r[b], (c, GLA_DK)) for b in range(nb)], axis=0)
    nt = lambda a, b: lax.dot_general(a, b, NT_DIMS, preferred_element_type=F32)

    for h in range(GLA_HEADS):
        dk = slice(h * GLA_DK, (h + 1) * GLA_DK)
        dv = slice(h * GLA_DV, (h + 1) * GLA_DV)
        cum = _cumsum_rows(tri_b, log_a_all[:, dk])
        r_start, r_end = {}, {}
        r_prev = jnp.zeros((1, GLA_DK), F32)
        for b in order:
            e = b * c if reverse else b * c + c - 1
            r_start[b] = r_prev
            r_end[b] = cum[e:e + 1]
            r_prev = r_end[b]
        total = r_prev
        mid = r_end[order[nb // 2 - 1]]
        cumloc = cum - per_chunk(r_start)
        to_end = per_chunk(r_end) - cum

        q = q_ref[:, dk] * (GLA_DK ** -0.5)
        k = k_ref[:, dk]
        v = v_ref[:, dv].astype(BF16)
        q_loc = (q * jnp.exp(cumloc)).astype(BF16)
        k_loc = (k * jnp.exp(-cumloc)).astype(BF16)
        k_end = (k * jnp.exp(to_end)).astype(BF16)
        q_mid = (q * jnp.exp(jnp.minimum(cum - mid, 0.0))).astype(BF16)
        k_mid = (k * jnp.exp(jnp.minimum(mid - cum, 0.0))).astype(BF16)
        q_all = (q * jnp.exp(cum)).astype(BF16)
        k_all = (k * jnp.exp(total - cum)).astype(BF16)

        attn = jnp.where(m_chunk, nt(q_loc, k_loc),
                         jnp.where(m_pair, nt(q_loc, k_end),
                                   jnp.where(m_cross, nt(q_mid, k_mid), 0.0))).astype(BF16)
        state = st_ref[h]
        o = jnp.dot(attn, v, preferred_element_type=F32) + nt(q_all, state.astype(BF16))
        st_ref[h] = state * jnp.exp(total) + lax.dot_general(
            v, k_all, TN_DIMS, preferred_element_type=F32)
        if final:
            tot = fwd_ref[:, dv] + o
            gr = gr_ref[:, dv]
            o_ref[:, dv] = (_rms_norm(tot, ng_ref[...]) * (gr * _sigmoid(gr))).astype(o_ref.dtype)
        else:
            o_ref[:, dv] = o


def _gla_direction(p, p_glr, offs, w2cat, gbcat, groups, *, reverse, fwd=None, norm_g=None):
    t = p.shape[0]
    final = fwd is not None
    lb = GLA_SUPER
    (_, _, seq_a), (off_b, _, seq_b) = groups
    n_a, ns_a, ns_b = off_b // lb, seq_a // lb, seq_b // lb
    assert seq_a % lb == 0 and seq_b % lb == 0
    d = 1 if reverse else 0

    def local(u):
        in_a = u < n_a
        return jnp.where(in_a, u % ns_a, (u - n_a) % ns_b), jnp.where(in_a, ns_a, ns_b)

    def rows(u):
        n, ns = local(u)
        return (u - n) + (ns - 1 - n) if reverse else u

    is_first = lambda u: local(u)[0] == 0
    key, val = GLA_HEADS * GLA_DK, GLA_HEADS * GLA_DV
    assert offs["gq"] % key == 0 and offs["gk"] % key == 0
    assert offs["gv"] % val == 0 and offs["gr"] % val == 0
    col = lambda off, width: (lambda u: (rows(u), off // width))
    in_specs = [pl.BlockSpec((lb, key), col(offs["gq"], key)),
                pl.BlockSpec((lb, key), col(offs["gk"], key)),
                pl.BlockSpec((lb, val), col(offs["gv"], val)),
                pl.BlockSpec((lb, LANES), col(0, LANES)),
                pl.BlockSpec((LANES, key), lambda u: (0, d)),
                pl.BlockSpec((1, key), lambda u: (0, d))]
    args = [p, p, p, p_glr, w2cat, gbcat]
    if final:
        in_specs += [pl.BlockSpec((lb, val), col(0, val)),
                     pl.BlockSpec((lb, val), col(offs["gr"], val)),
                     pl.BlockSpec((1, GLA_DV), lambda u: (0, 0))]
        args += [fwd, p, norm_g.reshape(1, GLA_DV)]
    return pl.pallas_call(
        functools.partial(_gla_kernel, reverse=reverse, is_first=is_first, final=final),
        grid=(t // lb,),
        in_specs=in_specs,
        out_specs=pl.BlockSpec((lb, val), col(0, val)),
        out_shape=jax.ShapeDtypeStruct((t, val), BF16 if final else F32),
        scratch_shapes=[pltpu.VMEM((GLA_HEADS, GLA_DV, GLA_DK), F32)],
        compiler_params=_params(1),
        name="gla_bwd" if reverse else "gla_fwd",
    )(*args)


def _merge_kernel(a1_ref, a2_ref, w1_ref, w2_ref, g1_ref, g2_ref, b1_ref, b2_ref, o_ref,
                  w1b_ref, w2b_ref):
    @pl.when(pl.program_id(1) == 0)
    def _():
        w1b_ref[...] = w1_ref[...].astype(BF16)
        w2b_ref[...] = w2_ref[...].astype(BF16)

    y1 = jnp.dot(a1_ref[...], w1b_ref[...], preferred_element_type=F32)
    y2 = jnp.dot(a2_ref[...], w2b_ref[...], preferred_element_type=F32)
    s1 = _sigmoid(g1_ref[...] + b1_ref[...])
    s2 = _sigmoid(g2_ref[...] + b2_ref[...])
    o_ref[...] = (s1 * y1 + s2 * y2).astype(o_ref.dtype)


def _branch_merge(o_mla, o_gla, w_mla, w_gla, p, gm_off, b_merge):
    t, k1 = o_mla.shape
    k2 = o_gla.shape[1]
    d = w_mla.shape[1]
    bm, bn = _block(t, 1024), _block(d, 512)
    g0, nd = gm_off // bn, d // bn
    return pl.pallas_call(
        _merge_kernel,
        grid=(nd, t // bm),
        in_specs=[pl.BlockSpec((bm, k1), lambda j, i: (i, 0)),
                  pl.BlockSpec((bm, k2), lambda j, i: (i, 0)),
                  pl.BlockSpec((k1, bn), lambda j, i: (0, j)),
                  pl.BlockSpec((k2, bn), lambda j, i: (0, j)),
                  pl.BlockSpec((bm, bn), lambda j, i: (i, g0 + j)),
                  pl.BlockSpec((bm, bn), lambda j, i: (i, g0 + nd + j)),
                  pl.BlockSpec((1, bn), lambda j, i: (0, j)),
                  pl.BlockSpec((1, bn), lambda j, i: (0, nd + j))],
        out_specs=pl.BlockSpec((bm, bn), lambda j, i: (i, j)),
        out_shape=jax.ShapeDtypeStruct((t, d), BF16),
        scratch_shapes=[pltpu.VMEM((k1, bn), BF16), pltpu.VMEM((k2, bn), BF16)],
        compiler_params=_params(2),
        name="branch_merge",
    )(o_mla, o_gla, w_mla, w_gla, p, p, b_merge.reshape(1, 2 * d), b_merge.reshape(1, 2 * d))


def _xattn_kernel(q_ref, k_ref, v_ref, o_ref, *, scale):
    s = lax.dot_general(q_ref[...], k_ref[...], NT_DIMS, preferred_element_type=F32) * scale
    o_ref[...] = _softmax_pv(s, v_ref[...]).astype(o_ref.dtype)


def _cross_attention(q, kv, groups, mem_groups, n_mem):
    t, d = q.shape
    hd = d // XA_HEADS
    out = jnp.zeros((t, d), BF16)
    for (row_off, batch, seq), mem_off in zip(groups, mem_groups):
        bq = _block(seq, 512)
        nq = seq // bq
        q_row = lambda b, i, h, nq=nq, o=row_off // bq: (o + b * nq + i, h)
        out = _group_call(
            functools.partial(_xattn_kernel, scale=hd ** -0.5),
            grid=(batch, nq, XA_HEADS),
            in_specs=[pl.BlockSpec((bq, hd), q_row),
                      pl.BlockSpec((n_mem, hd), lambda b, i, h, o=mem_off // n_mem: (o + b, h)),
                      pl.BlockSpec((n_mem, hd),
                                   lambda b, i, h, o=mem_off // n_mem: (o + b, XA_HEADS + h))],
            out_spec=pl.BlockSpec((bq, hd), q_row),
            out=out, args=(q, kv, kv), name="cross_attention")
    return out


def _in_proj_layout(d_model, q_lora, kv_lora):
    gla_key, gla_val = GLA_HEADS * GLA_DK, GLA_HEADS * GLA_DV
    b_width = 2 * gla_key + 2 * gla_val
    s_kr = q_lora + kv_lora
    s_gq = s_kr + MLA_ROPE
    s_glr = s_gq + b_width
    s_gm = s_glr + 2 * GLA_GATE_RANK
    pb = 1024 if b_width % 1024 == 0 and (2 * d_model) % 1024 == 0 else 512
    assert b_width % pb == 0 and (2 * d_model) % pb == 0
    assert s_kr % LANES == 0 and q_lora % kv_lora == 0 and 2 * GLA_GATE_RANK <= LANES
    assert s_gq % 8 == 0 and s_glr % 8 == 0 and s_gm % 8 == 0
    n_a = -(-s_gq // pb)
    gq = n_a * pb
    starts = ([pb * j for j in range(n_a)] + [s_gq + pb * j for j in range(b_width // pb)]
              + [s_gm + pb * j for j in range(2 * d_model // pb)])
    offs = {"cq": 0, "ckv": q_lora, "krope": s_kr, "gq": gq, "gk": gq + gla_key,
            "gv": gq + 2 * gla_key, "gr": gq + 2 * gla_key + gla_val, "gm": gq + b_width}
    return offs, starts, pb, s_glr


def _in_proj_kernel(a_ref, wt_ref, o_ref):
    o_ref[...] = lax.dot_general(a_ref[...], wt_ref[...], NT_DIMS, preferred_element_type=F32)


def _in_proj(a, w_t, starts, bn, *, bm, name):
    m, kdim = a.shape
    bm = _block(m, bm)

    sub = 16
    assert all(s % sub == 0 for s in starts)

    def row_start(j):
        r = jnp.int32(starts[-1] // sub)
        for idx in range(len(starts) - 2, -1, -1):
            r = jnp.where(j == idx, jnp.int32(starts[idx] // sub), r)
        return pl.multiple_of(r * sub, sub)

    return pl.pallas_call(
        _in_proj_kernel,
        grid=(m // bm, len(starts)),
        in_specs=[pl.BlockSpec((bm, kdim), lambda i, j: (i, 0)),
                  pl.BlockSpec((pl.Element(bn), pl.Element(kdim)),
                               lambda i, j: (row_start(j), 0))],
        out_specs=pl.BlockSpec((bm, bn), lambda i, j: (i, j)),
        out_shape=jax.ShapeDtypeStruct((m, bn * len(starts)), F32),
        compiler_params=_params(2),
        name=name,
    )(a, w_t)


def _pack_uq(w_uq):
    q_lora = w_uq.shape[0]
    w = w_uq.reshape(q_lora, MLA_HEADS, MLA_NOPE + MLA_ROPE)
    rope = w[:, :, MLA_NOPE:]
    half = MLA_ROPE // 2
    rot = jnp.concatenate([-rope[:, :, half:], rope[:, :, :half]], axis=2)
    return jnp.concatenate([w, rot], axis=2).reshape(q_lora, MLA_HEADS * MLA_QK_PAD).astype(BF16)


def _pack_ukv(w_ukv):
    kv_lora = w_ukv.shape[0]
    w = w_ukv.reshape(kv_lora, MLA_HEADS, MLA_NOPE + MLA_V)
    wk = w[:, :, :MLA_NOPE].reshape(kv_lora, MLA_HEADS * MLA_NOPE)
    wv = w[:, :, MLA_NOPE:].reshape(kv_lora, MLA_HEADS * MLA_V)
    return wk.astype(BF16), wv.astype(BF16)


def _pack_gate(w2, gb):
    r, key = w2.shape[1], w2.shape[2]
    z = jnp.zeros((r, key), w2.dtype)
    top = jnp.concatenate([w2[0], z], axis=1)
    bot = jnp.concatenate([z, w2[1]], axis=1)
    tail = jnp.zeros((LANES - 2 * r, 2 * key), w2.dtype)
    return jnp.concatenate([top, bot, tail], axis=0).astype(BF16), gb.reshape(1, 2 * key)


def kernel(x_prompt, x_sample, mem_prompt, mem_sample, ln_in_g, ln_in_b, w_in, b_merge, mla_q_norm, w_uq, mla_kv_norm, w_ukv, gla_gate_w2, gla_gate_b, gla_norm, w_branch_mla, w_branch_gla, w_mix_out, ln1_g, ln1_b, xa_wq, xa_wkv, xa_wo, ln2_g, ln2_b, mlp_w1, mlp_w2, ln3_g, ln3_b):
    assert w_in.shape[0] == DEPTH
    ba, sa, d = x_prompt.shape
    bb, sb, _ = x_sample.shape
    n_mem = mem_prompt.shape[1]
    ta, tb = ba * sa, bb * sb
    groups = ((0, ba, sa), (ta, bb, sb))
    mem_groups = (0, ba * n_mem)
    q_lora, kv_lora = mla_q_norm.shape[1], mla_kv_norm.shape[1]
    assert ta % sb == 0 and ta % 1024 == 0 and sa % 1024 == 0 and sb % 1024 == 0

    offs, in_starts, in_bn, glr_start = _in_proj_layout(d, q_lora, kv_lora)
    w_in_t = jnp.swapaxes(w_in[0], 0, 1).astype(BF16)
    wq_p = _pack_uq(w_uq[0])
    wk_p, wv_p = _pack_ukv(w_ukv[0])
    w2cat, gbcat = _pack_gate(gla_gate_w2[0], gla_gate_b[0])
    cos, sin = _rope_tables(max(sa, sb))

    h, h_b = _ln_in(x_prompt.reshape(ta, d), x_sample.reshape(tb, d), ln_in_g, ln_in_b)

    p = _in_proj(h_b, w_in_t, in_starts, in_bn, bm=1024, name="in_proj")
    p_glr = _in_proj(h_b, w_in_t, [glr_start], LANES, bm=2048, name="in_proj_gate")
    q = _q_proj(p, offs["cq"], q_lora, mla_q_norm[0], wq_p, cos, sin, groups, 1024)
    k, v = _kv_proj(p, offs["ckv"], kv_lora, offs["krope"], mla_kv_norm[0], wk_p, wv_p,
                    cos, sin, groups, 512)
    o_mla = _mla_attention(q, k, v, groups)
    gla_f = _gla_direction(p, p_glr, offs, w2cat, gbcat, groups, reverse=False)
    o_gla = _gla_direction(p, p_glr, offs, w2cat, gbcat, groups, reverse=True, fwd=gla_f,
                           norm_g=gla_norm[0])
    merged = _branch_merge(o_mla, o_gla, w_branch_mla[0], w_branch_gla[0], p, offs["gm"],
                           b_merge[0])
    h, h_b = _matmul_ln(merged, w_mix_out[0].astype(BF16), h, ln1_g[0], ln1_b[0], bm=256,
                        name="mix_out_ln")

    mem = jnp.concatenate([mem_prompt.reshape(ba * n_mem, d), mem_sample.reshape(bb * n_mem, d)])
    xq = _matmul_wres(h_b, xa_wq[0], BF16, name="xa_q")
    xkv = _matmul_wres(mem.astype(BF16), xa_wkv[0], BF16, bm=2048, bn=512, name="xa_kv")
    o_x = _cross_attention(xq, xkv, groups, mem_groups, n_mem)
    h, h_b = _matmul_ln(o_x, xa_wo[0].astype(BF16), h, ln2_g[0], ln2_b[0], bm=256,
                        name="xa_o_ln")

    u = _matmul_wres(h_b, mlp_w1[0], BF16, relu2=True, name="mlp_up")
    ff = _matmul(u, mlp_w2[0].astype(BF16), F32, name="mlp_down")
    y_a, = _res_ln(h, ff, ln3_g[0], ln3_b[0], (F32,), row_off=0, rows=ta)
    y_b, = _res_ln(h, ff, ln3_g[0], ln3_b[0], (F32,), row_off=ta, rows=tb)
    return y_a.reshape(ba, sa, d), y_b.reshape(bb, sb, d)
```

```python
import functools

import numpy as np
import jax
import jax.numpy as jnp
from jax import lax
from jax.experimental import pallas as pl
from jax.experimental.pallas import tpu as pltpu

MLA_HEADS = 16
MLA_NOPE = 128
MLA_ROPE = 64
MLA_V = 128
ROPE_THETA = 10000.0
GLA_HEADS = 4
GLA_DK = 256
GLA_DV = 512
GLA_GATE_RANK = 16
GLA_GATE_NORM = 16.0
GLA_CHUNK = 64
XA_HEADS = 4
LN_EPS = 1e-5
RMS_EPS = 1e-6
DEPTH = 1
DN_ALPHA = (2.0 * DEPTH) ** 0.25

LANES = 128
MLA_QK_PAD = 2 * LANES
MLA_V_PAD = 2 * LANES
MLA_Q_ROWS = 256
VMEM_LIMIT_BYTES = 56 * 2**20
VMEM_LIMIT_LARGE_BYTES = 63 * 2**20

BF16 = jnp.bfloat16
F32 = jnp.float32
NT_DIMS = (((1,), (1,)), ((), ()))
TN_DIMS = (((0,), (0,)), ((), ()))


def _params(n_grid, vmem_limit_bytes=VMEM_LIMIT_BYTES):
    return pltpu.CompilerParams(dimension_semantics=("arbitrary",) * n_grid,
                                vmem_limit_bytes=vmem_limit_bytes)


def _block(n, pref):
    b = min(n, pref)
    while n % b:
        b //= 2
    return b


def _round_up(n, m):
    return -(-n // m) * m


def _layer_norm(x, g, b):
    mu = jnp.mean(x, -1, keepdims=True)
    xc = x - mu
    var = jnp.mean(xc * xc, -1, keepdims=True)
    return xc * lax.rsqrt(var + LN_EPS) * g + b


def _rms_norm(x, g):
    return x * lax.rsqrt(jnp.mean(x * x, -1, keepdims=True) + RMS_EPS) * g


def _sigmoid(x):
    return 1.0 / (1.0 + jnp.exp(-x))


def _rope(x, cos, sin):
    return x * cos + pltpu.roll(x, MLA_ROPE, 1) * sin


def _rope_rotate(x, cos, sin):
    half = MLA_ROPE // 2
    lane = lax.broadcasted_iota(jnp.int32, x.shape, 1)
    rot = jnp.where(lane < half, -pltpu.roll(x, LANES - half, 1), pltpu.roll(x, half, 1))
    return x * cos + rot * sin


def _ln_in_kernel(xa_ref, xb_ref, g_ref, b_ref, h_ref, hb_ref, *, n_a):
    def emit(x_ref):
        y = _layer_norm(x_ref[...], g_ref[...], b_ref[...])
        h_ref[...] = y
        hb_ref[...] = y.astype(BF16)

    @pl.when(pl.program_id(0) < n_a)
    def _():
        emit(xa_ref)

    @pl.when(pl.program_id(0) >= n_a)
    def _():
        emit(xb_ref)


def _ln_in(xa, xb, g, b):
    ta, d = xa.shape
    tb = xb.shape[0]
    bm = _block(np.gcd(ta, tb), 256)
    n_a, n_b = ta // bm, tb // bm
    row = pl.BlockSpec((1, d), lambda i: (0, 0))
    out = pl.BlockSpec((bm, d), lambda i: (i, 0))
    return pl.pallas_call(
        functools.partial(_ln_in_kernel, n_a=n_a),
        grid=(n_a + n_b,),
        in_specs=[pl.BlockSpec((bm, d), lambda i: (jnp.minimum(i, n_a - 1), 0)),
                  pl.BlockSpec((bm, d), lambda i: (jnp.maximum(i - n_a, 0), 0)),
                  row, row],
        out_specs=[out, out],
        out_shape=[jax.ShapeDtypeStruct((ta + tb, d), F32),
                   jax.ShapeDtypeStruct((ta + tb, d), BF16)],
        compiler_params=_params(1),
        name="ln_in",
    )(xa, xb, g.reshape(1, d), b.reshape(1, d))


def _ln_rows_kernel(z_ref, g_ref, b_ref, o_ref):
    o_ref[...] = _layer_norm(z_ref[...], g_ref[...], b_ref[...])


def _ln_rows(z, g, b, *, row_off, rows):
    d = z.shape[1]
    bm = _block(np.gcd(rows, row_off) if row_off else rows, 256)
    off = row_off // bm
    row = pl.BlockSpec((1, d), lambda i: (0, 0))
    return pl.pallas_call(
        _ln_rows_kernel,
        grid=(rows // bm,),
        in_specs=[pl.BlockSpec((bm, d), lambda i: (i + off, 0)), row, row],
        out_specs=pl.BlockSpec((bm, d), lambda i: (i, 0)),
        out_shape=jax.ShapeDtypeStruct((rows, d), F32),
        compiler_params=_params(1),
        name="ln_out",
    )(z, g.reshape(1, d), b.reshape(1, d))


def _matmul_res_kernel(a_ref, b_ref, h_ref, o_ref, *scratch, nk):
    part = jnp.dot(a_ref[...], b_ref[...], preferred_element_type=F32)

    def finish(acc):
        o_ref[...] = DN_ALPHA * h_ref[...] + acc

    if nk == 1:
        finish(part)
        return
    acc_ref, = scratch
    k = pl.program_id(2)

    @pl.when(k == 0)
    def _():
        acc_ref[...] = part

    @pl.when(jnp.logical_and(k > 0, k < nk - 1))
    def _():
        acc_ref[...] += part

    @pl.when(k == nk - 1)
    def _():
        finish(acc_ref[...] + part)


def _matmul_res(a, b, h, *, bm=1024, bn=1024, bk=4096, name="matmul_res"):
    m, kdim = a.shape
    n = b.shape[1]
    bm, bn, bk = _block(m, bm), _block(n, bn), _block(kdim, bk)
    nk = kdim // bk
    tile = pl.BlockSpec((bm, bn), lambda i, j, k: (i, j))
    return pl.pallas_call(
        functools.partial(_matmul_res_kernel, nk=nk),
        grid=(m // bm, n // bn, nk),
        in_specs=[pl.BlockSpec((bm, bk), lambda i, j, k: (i, k)),
                  pl.BlockSpec((bk, bn), lambda i, j, k: (k, j)),
                  tile],
        out_specs=tile,
        out_shape=jax.ShapeDtypeStruct((m, n), F32),
        scratch_shapes=[pltpu.VMEM((bm, bn), F32)] if nk > 1 else [],
        compiler_params=_params(3, VMEM_LIMIT_LARGE_BYTES),
        name=name,
    )(a, b, h)


def _matmul_wres_kernel(a_ref, w_ref, o_ref, wb_ref, *, relu2):
    @pl.when(pl.program_id(1) == 0)
    def _():
        wb_ref[...] = w_ref[...].astype(BF16)

    acc = jnp.dot(a_ref[...], wb_ref[...], preferred_element_type=F32)
    if relu2:
        acc = jnp.square(jnp.maximum(acc, 0.0))
    o_ref[...] = acc.astype(o_ref.dtype)


def _matmul_wres(a, w, out_dtype, *, bm=512, bn=1024, relu2=False, name="matmul_wres"):
    m, kdim = a.shape
    n = w.shape[1]
    bm, bn = _block(m, bm), _block(n, bn)
    return pl.pallas_call(
        functools.partial(_matmul_wres_kernel, relu2=relu2),
        grid=(n // bn, m // bm),
        in_specs=[pl.BlockSpec((bm, kdim), lambda j, i: (i, 0)),
                  pl.BlockSpec((kdim, bn), lambda j, i: (0, j))],
        out_specs=pl.BlockSpec((bm, bn), lambda j, i: (i, j)),
        out_shape=jax.ShapeDtypeStruct((m, n), out_dtype),
        scratch_shapes=[pltpu.VMEM((kdim, bn), BF16)],
        compiler_params=_params(2),
        name=name,
    )(a, w)


def _matmul_ln_kernel(a_ref, w_ref, h_ref, g_ref, b_ref, o_ref, ob_ref, *, chains):
    rows_per = a_ref.shape[0] // chains
    for c in range(chains):
        rows = pl.ds(c * rows_per, rows_per)
        y = jnp.dot(a_ref[rows, :], w_ref[...], preferred_element_type=F32)
        z = _layer_norm(DN_ALPHA * h_ref[rows, :] + y, g_ref[...], b_ref[...])
        o_ref[rows, :] = z
        ob_ref[rows, :] = z.astype(BF16)


def _matmul_ln(a, w, h, g, b, *, bm, name):
    m, kdim = a.shape
    n = w.shape[1]
    bm = _block(m, bm)
    row = pl.BlockSpec((1, n), lambda i: (0, 0))
    tile = pl.BlockSpec((bm, n), lambda i: (i, 0))
    return pl.pallas_call(
        functools.partial(_matmul_ln_kernel, chains=2 if bm % 256 == 0 else 1),
        grid=(m // bm,),
        in_specs=[pl.BlockSpec((bm, kdim), lambda i: (i, 0)),
                  pl.BlockSpec((kdim, n), lambda i: (0, 0), pipeline_mode=pl.Buffered(1)),
                  tile, row, row],
        out_specs=[tile, tile],
        out_shape=[jax.ShapeDtypeStruct((m, n), F32), jax.ShapeDtypeStruct((m, n), BF16)],
        compiler_params=_params(1, VMEM_LIMIT_LARGE_BYTES),
        name=name,
    )(a, w, h, g.reshape(1, n), b.reshape(1, n))


def _rope_tables(seq_len):
    half = MLA_ROPE // 2
    inv = 1.0 / (ROPE_THETA ** (jnp.arange(0, MLA_ROPE, 2, dtype=F32) / MLA_ROPE))
    ang = jnp.arange(seq_len, dtype=F32)[:, None] * inv[None, :]
    zero = jnp.zeros((seq_len, LANES - 2 * half), F32)
    cos = jnp.concatenate([jnp.cos(ang), jnp.cos(ang), zero], axis=1)
    sin = jnp.concatenate([jnp.sin(ang), jnp.sin(ang), zero], axis=1)
    return cos, sin


def _pos_block_map(groups, bm):
    (_, _, s_a), (off_b, _, s_b) = groups
    n_a = off_b // bm

    def index(i):
        return jnp.where(i < n_a, i % (s_a // bm), (i - n_a) % (s_b // bm))
    return index


def _qproj_kernel(c_ref, g_ref, w_ref, cos_ref, sin_ref, o_ref, xn_ref, *, heads, scale):
    @pl.when(pl.program_id(1) == 0)
    def _():
        xn_ref[...] = _rms_norm(c_ref[...], g_ref[...]).astype(BF16)

    r = jnp.dot(xn_ref[...], w_ref[...], preferred_element_type=F32)
    cos, sin = cos_ref[...], sin_ref[...]
    for h in range(heads):
        lo = h * MLA_QK_PAD
        o_ref[:, lo:lo + MLA_NOPE] = (r[:, lo:lo + MLA_NOPE] * scale).astype(BF16)
        o_ref[:, lo + MLA_NOPE:lo + MLA_QK_PAD] = (
            _rope(r[:, lo + MLA_NOPE:lo + MLA_QK_PAD], cos, sin) * scale).astype(BF16)


def _q_proj(p, c_off, q_lora, g, wq, cos, sin, groups, bm):
    t = p.shape[0]
    heads = min(8, MLA_HEADS)
    bn = heads * MLA_QK_PAD
    pos = _pos_block_map(groups, bm)
    scale = (MLA_NOPE + MLA_ROPE) ** -0.5 * np.log2(np.e)
    return pl.pallas_call(
        functools.partial(_qproj_kernel, heads=heads, scale=scale),
        grid=(t // bm, MLA_HEADS // heads),
        in_specs=[pl.BlockSpec((bm, q_lora), lambda i, j: (i, c_off // q_lora)),
                  pl.BlockSpec((1, q_lora), lambda i, j: (0, 0)),
                  pl.BlockSpec((q_lora, bn), lambda i, j: (0, j)),
                  pl.BlockSpec((bm, LANES), lambda i, j: (pos(i), 0)),
                  pl.BlockSpec((bm, LANES), lambda i, j: (pos(i), 0))],
        out_specs=pl.BlockSpec((bm, bn), lambda i, j: (i, j)),
        out_shape=jax.ShapeDtypeStruct((t, MLA_HEADS * MLA_QK_PAD), BF16),
        scratch_shapes=[pltpu.VMEM((bm, q_lora), BF16)],
        compiler_params=_params(2),
        name="mla_q_proj",
    )(p, g.reshape(1, q_lora), wq, cos, sin)


def _kvproj_kernel(c_ref, kr_ref, g_ref, wk_ref, wv_ref, cos_ref, sin_ref, k_ref, v_ref):
    xn = _rms_norm(c_ref[...], g_ref[...]).astype(BF16)
    kn = jnp.dot(xn, wk_ref[...], preferred_element_type=F32)
    vn = jnp.dot(xn, wv_ref[...], preferred_element_type=F32)
    k_rope = _rope_rotate(kr_ref[...], cos_ref[...], sin_ref[...]).astype(BF16)
    ones = jnp.ones((k_ref.shape[0], MLA_V_PAD - MLA_V), BF16)
    for h in range(MLA_HEADS):
        lo = h * MLA_QK_PAD
        k_ref[:, lo:lo + MLA_NOPE] = kn[:, h * MLA_NOPE:(h + 1) * MLA_NOPE].astype(BF16)
        k_ref[:, lo + MLA_NOPE:lo + MLA_QK_PAD] = k_rope
        lo = h * MLA_V_PAD
        v_ref[:, lo:lo + MLA_V] = vn[:, h * MLA_V:(h + 1) * MLA_V].astype(BF16)
        v_ref[:, lo + MLA_V:lo + MLA_V_PAD] = ones


def _kv_proj(p, c_off, kv_lora, kr_off, g, wk, wv, cos, sin, groups, bm):
    t = p.shape[0]
    pos = _pos_block_map(groups, bm)
    nk, nv = MLA_HEADS * MLA_QK_PAD, MLA_HEADS * MLA_V
    return pl.pallas_call(
        _kvproj_kernel,
        grid=(t // bm,),
        in_specs=[pl.BlockSpec((bm, kv_lora), lambda i: (i, c_off // kv_lora)),
                  pl.BlockSpec((bm, LANES), lambda i: (i, kr_off // LANES)),
                  pl.BlockSpec((1, kv_lora), lambda i: (0, 0)),
                  pl.BlockSpec((kv_lora, MLA_HEADS * MLA_NOPE), lambda i: (0, 0)),
                  pl.BlockSpec((kv_lora, nv), lambda i: (0, 0)),
                  pl.BlockSpec((bm, LANES), lambda i: (pos(i), 0)),
                  pl.BlockSpec((bm, LANES), lambda i: (pos(i), 0))],
        out_specs=[pl.BlockSpec((bm, nk), lambda i: (i, 0)),
                   pl.BlockSpec((bm, MLA_HEADS * MLA_V_PAD), lambda i: (i, 0))],
        out_shape=[jax.ShapeDtypeStruct((t, nk), BF16),
                   jax.ShapeDtypeStruct((t, MLA_HEADS * MLA_V_PAD), BF16)],
        compiler_params=_params(1),
        name="mla_kv_proj",
    )(p, p, g.reshape(1, kv_lora), wk, wv, cos, sin)


def _group_call(kernel, *, grid, in_specs, out_spec, out, args, scratch_shapes=(), name):
    def body(*refs):
        n_in = len(in_specs)
        kernel(*refs[:n_in], *refs[n_in + 1:])

    return pl.pallas_call(
        body,
        grid=grid,
        in_specs=list(in_specs) + [pl.BlockSpec(memory_space=pl.ANY)],
        out_specs=out_spec,
        out_shape=jax.ShapeDtypeStruct(out.shape, out.dtype),
        input_output_aliases={len(in_specs): 0},
        scratch_shapes=list(scratch_shapes),
        compiler_params=_params(len(grid)),
        name=name,
    )(*args, out)


def _softmax_pv(s, v):
    m = jnp.max(s, -1, keepdims=True)
    e = jnp.exp(s - m)
    l = jnp.sum(e, -1, keepdims=True)
    return jnp.dot(e.astype(BF16), v, preferred_element_type=F32) / l


def _mla_attn_kernel(q_ref, k_ref, v_ref, o_ref, *, chains):
    for c in range(chains):
        rows = pl.ds(c * MLA_Q_ROWS, MLA_Q_ROWS)
        s = lax.dot_general(q_ref[rows, :], k_ref[...], NT_DIMS, preferred_element_type=F32)
        e = jnp.exp2(s - jnp.max(s, -1, keepdims=True)).astype(BF16)
        acc = jnp.dot(e, v_ref[...], preferred_element_type=F32)
        o_ref[rows, :] = (acc[:, :MLA_V] / acc[:, MLA_V:]).astype(o_ref.dtype)


def _mla_attention(q, k, v, groups):
    t = q.shape[0]
    out = jnp.zeros((t, MLA_HEADS * MLA_V), BF16)
    for row_off, batch, seq in groups:
        bq = _block(seq, 8 * MLA_Q_ROWS)
        nq = seq // bq
        q_row = lambda b, h, i, nq=nq, o=row_off // bq: (o + b * nq + i, h)
        kv_row = lambda b, h, i, o=row_off // seq: (o + b, h)
        out = _group_call(
            functools.partial(_mla_attn_kernel, chains=bq // MLA_Q_ROWS),
            grid=(batch, MLA_HEADS, nq),
            in_specs=[pl.BlockSpec((bq, MLA_QK_PAD), q_row),
                      pl.BlockSpec((seq, MLA_QK_PAD), kv_row),
                      pl.BlockSpec((seq, MLA_V_PAD), kv_row)],
            out_spec=pl.BlockSpec((bq, MLA_V), q_row),
            out=out, args=(q, k, v), name="mla_attention")
    return out


GLA_SUPER = 4 * GLA_CHUNK


def _cumsum_rows(tri_b, x):
    hi = x.astype(BF16)
    r1 = x - hi.astype(F32)
    mid = r1.astype(BF16)
    lo = (r1 - mid.astype(F32)).astype(BF16)
    dot = lambda t: jnp.dot(tri_b, t, preferred_element_type=F32)
    return dot(hi) + dot(mid) + dot(lo)


def _gla_kernel(*refs, reverse, is_first, final):
    if final:
        (q_ref, k_ref, v_ref, lr_ref, w2_ref, gb_ref, fwd_ref, gr_ref, ng_ref,
         o_ref, st_ref) = refs
    else:
        q_ref, k_ref, v_ref, lr_ref, w2_ref, gb_ref, o_ref, st_ref = refs
    c, n_rows = GLA_CHUNK, GLA_SUPER
    nb = n_rows // c

    @pl.when(is_first(pl.program_id(0)))
    def _():
        st_ref[...] = jnp.zeros_like(st_ref)

    gate = jnp.dot(lr_ref[...].astype(BF16), w2_ref[...], preferred_element_type=F32) + gb_ref[...]
    log_a_all = ((jnp.minimum(gate, 0.0) - jnp.log1p(jnp.exp(-jnp.abs(gate))))
                 * (1.0 / GLA_GATE_NORM))

    row = lax.broadcasted_iota(jnp.int32, (n_rows, n_rows), 0)
    col = lax.broadcasted_iota(jnp.int32, (n_rows, n_rows), 1)
    before = (col > row) if reverse else (col < row)
    upto = (col >= row) if reverse else (col <= row)
    same_chunk = (row // c) == (col // c)
    same_pair = (row // (2 * c)) == (col // (2 * c))
    m_chunk = jnp.logical_and(same_chunk, upto)
    m_pair = jnp.logical_and(jnp.logical_and(same_pair, jnp.logical_not(same_chunk)), before)
    m_cross = jnp.logical_and(jnp.logical_not(same_pair), before)

    tri_b = upto.astype(BF16)
    order = list(reversed(range(nb))) if reverse else list(range(nb))
    per_chunk = lambda r: jnp.concatenate(
        [jnp.broadcast_to(r[b], (c, GLA_DK)) for b in range(nb)], axis=0)
    nt = lambda a, b: lax.dot_general(a, b, NT_DIMS, preferred_element_type=F32)

    for h in range(GLA_HEADS):
        dk = slice(h * GLA_DK, (h + 1) * GLA_DK)
        dv = slice(h * GLA_DV, (h + 1) * GLA_DV)
        cum = _cumsum_rows(tri_b, log_a_all[:, dk])
        r_start, r_end = {}, {}
        r_prev = jnp.zeros((1, GLA_DK), F32)
        for b in order:
            e = b * c if reverse else b * c + c - 1
            r_start[b] = r_prev
            r_end[b] = cum[e:e + 1]
            r_prev = r_end[b]
        total = r_prev
        mid = r_end[order[nb // 2 - 1]]
        cumloc = cum - per_chunk(r_start)
        to_end = per_chunk(r_end) - cum

        q = q_ref[:, dk] * (GLA_DK ** -0.5)
        k = k_ref[:, dk]
        v = v_ref[:, dv].astype(BF16)
        q_loc = (q * jnp.exp(cumloc)).astype(BF16)
        k_loc = (k * jnp.exp(-cumloc)).astype(BF16)
        k_end = (k * jnp.exp(to_end)).astype(BF16)
        q_mid = (q * jnp.exp(jnp.minimum(cum - mid, 0.0))).astype(BF16)
        k_mid = (k * jnp.exp(jnp.minimum(mid - cum, 0.0))).astype(BF16)
        q_all = (q * jnp.exp(cum)).astype(BF16)
        k_all = (k * jnp.exp(total - cum)).astype(BF16)

        attn = jnp.where(m_chunk, nt(q_loc, k_loc),
                         jnp.where(m_pair, nt(q_loc, k_end),
                                   jnp.where(m_cross, nt(q_mid, k_mid), 0.0))).astype(BF16)
        state = st_ref[h]
        o = jnp.dot(attn, v, preferred_element_type=F32) + nt(q_all, state.astype(BF16))
        st_ref[h] = state * jnp.exp(total) + lax.dot_general(
            v, k_all, TN_DIMS, preferred_element_type=F32)
        if final:
            tot = fwd_ref[:, dv] + o
            gr = gr_ref[:, dv]
            o_ref[:, dv] = (_rms_norm(tot, ng_ref[...]) * (gr * _sigmoid(gr))).astype(o_ref.dtype)
        else:
            o_ref[:, dv] = o


def _gla_direction(p, p_glr, offs, w2cat, gbcat, groups, *, reverse, fwd=None, norm_g=None):
    t = p.shape[0]
    final = fwd is not None
    lb = GLA_SUPER
    (_, _, seq_a), (off_b, _, seq_b) = groups
    n_a, ns_a, ns_b = off_b // lb, seq_a // lb, seq_b // lb
    assert seq_a % lb == 0 and seq_b % lb == 0
    d = 1 if reverse else 0

    def local(u):
        in_a = u < n_a
        return jnp.where(in_a, u % ns_a, (u - n_a) % ns_b), jnp.where(in_a, ns_a, ns_b)

    def rows(u):
        n, ns = local(u)
        return (u - n) + (ns - 1 - n) if reverse else u

    is_first = lambda u: local(u)[0] == 0
    key, val = GLA_HEADS * GLA_DK, GLA_HEADS * GLA_DV
    assert offs["gq"] % key == 0 and offs["gk"] % key == 0
    assert offs["gv"] % val == 0 and offs["gr"] % val == 0
    col = lambda off, width: (lambda u: (rows(u), off // width))
    in_specs = [pl.BlockSpec((lb, key), col(offs["gq"], key)),
                pl.BlockSpec((lb, key), col(offs["gk"], key)),
                pl.BlockSpec((lb, val), col(offs["gv"], val)),
                pl.BlockSpec((lb, LANES), col(0, LANES)),
                pl.BlockSpec((LANES, key), lambda u: (0, d)),
                pl.BlockSpec((1, key), lambda u: (0, d))]
    args = [p, p, p, p_glr, w2cat, gbcat]
    if final:
        in_specs += [pl.BlockSpec((lb, val), col(0, val)),
                     pl.BlockSpec((lb, val), col(offs["gr"], val)),
                     pl.BlockSpec((1, GLA_DV), lambda u: (0, 0))]
        args += [fwd, p, norm_g.reshape(1, GLA_DV)]
    return pl.pallas_call(
        functools.partial(_gla_kernel, reverse=reverse, is_first=is_first, final=final),
        grid=(t // lb,),
        in_specs=in_specs,
        out_specs=pl.BlockSpec((lb, val), col(0, val)),
        out_shape=jax.ShapeDtypeStruct((t, val), BF16 if final else F32),
        scratch_shapes=[pltpu.VMEM((GLA_HEADS, GLA_DV, GLA_DK), F32)],
        compiler_params=_params(1),
        name="gla_bwd" if reverse else "gla_fwd",
    )(*args)


def _merge_kernel(a1_ref, a2_ref, w1_ref, w2_ref, g1_ref, g2_ref, b1_ref, b2_ref, o_ref):
    y1 = jnp.dot(a1_ref[...], w1_ref[...], preferred_element_type=F32)
    y2 = jnp.dot(a2_ref[...], w2_ref[...], preferred_element_type=F32)
    s1 = _sigmoid(g1_ref[...] + b1_ref[...])
    s2 = _sigmoid(g2_ref[...] + b2_ref[...])
    o_ref[...] = (s1 * y1 + s2 * y2).astype(o_ref.dtype)


def _branch_merge(o_mla, o_gla, w_mla, w_gla, p, gm_off, b_merge):
    t, k1 = o_mla.shape
    k2 = o_gla.shape[1]
    d = w_mla.shape[1]
    bm, bn = _block(t, 1024), _block(d, 512)
    g0, nd = gm_off // bn, d // bn
    return pl.pallas_call(
        _merge_kernel,
        grid=(t // bm, nd),
        in_specs=[pl.BlockSpec((bm, k1), lambda i, j: (i, 0)),
                  pl.BlockSpec((bm, k2), lambda i, j: (i, 0)),
                  pl.BlockSpec((k1, bn), lambda i, j: (0, j)),
                  pl.BlockSpec((k2, bn), lambda i, j: (0, j)),
                  pl.BlockSpec((bm, bn), lambda i, j: (i, g0 + j)),
                  pl.BlockSpec((bm, bn), lambda i, j: (i, g0 + nd + j)),
                  pl.BlockSpec((1, bn), lambda i, j: (0, j)),
                  pl.BlockSpec((1, bn), lambda i, j: (0, nd + j))],
        out_specs=pl.BlockSpec((bm, bn), lambda i, j: (i, j)),
        out_shape=jax.ShapeDtypeStruct((t, d), BF16),
        compiler_params=_params(2),
        name="branch_merge",
    )(o_mla, o_gla, w_mla, w_gla, p, p, b_merge.reshape(1, 2 * d), b_merge.reshape(1, 2 * d))


def _xattn_kernel(q_ref, k_ref, v_ref, o_ref, *, scale):
    hd = q_ref.shape[1] // XA_HEADS
    for h in range(XA_HEADS):
        cols = slice(h * hd, (h + 1) * hd)
        s = lax.dot_general(q_ref[:, cols], k_ref[:, cols], NT_DIMS,
                            preferred_element_type=F32) * scale
        o_ref[:, cols] = _softmax_pv(s, v_ref[:, cols]).astype(o_ref.dtype)


def _cross_attention(q, kv, groups, n_mem):
    t, d = q.shape
    (_, batch_a, seq_a), (off_b, _, seq_b) = groups
    bq = _block(np.gcd(seq_a, seq_b), 512)
    n_a, per_a, per_b = off_b // bq, seq_a // bq, seq_b // bq
    mem_row = lambda u: jnp.where(u < n_a, u // per_a, batch_a + (u - n_a) // per_b)
    return pl.pallas_call(
        functools.partial(_xattn_kernel, scale=(d // XA_HEADS) ** -0.5),
        grid=(t // bq,),
        in_specs=[pl.BlockSpec((bq, d), lambda u: (u, 0)),
                  pl.BlockSpec((n_mem, d), lambda u: (mem_row(u), 0)),
                  pl.BlockSpec((n_mem, d), lambda u: (mem_row(u), 1))],
        out_specs=pl.BlockSpec((bq, d), lambda u: (u, 0)),
        out_shape=jax.ShapeDtypeStruct((t, d), BF16),
        compiler_params=_params(1),
        name="cross_attention",
    )(q, kv, kv)


def _in_proj_layout(d_model, q_lora, kv_lora):
    gla_key, gla_val = GLA_HEADS * GLA_DK, GLA_HEADS * GLA_DV
    b_width = 2 * gla_key + 2 * gla_val
    s_kr = q_lora + kv_lora
    s_gq = s_kr + MLA_ROPE
    s_glr = s_gq + b_width
    s_gm = s_glr + 2 * GLA_GATE_RANK
    pb = 1024 if b_width % 1024 == 0 and (2 * d_model) % 1024 == 0 else 512
    assert b_width % pb == 0 and (2 * d_model) % pb == 0
    assert s_kr % LANES == 0 and q_lora % kv_lora == 0 and 2 * GLA_GATE_RANK <= LANES
    assert s_gq % 8 == 0 and s_glr % 8 == 0 and s_gm % 8 == 0
    n_a = -(-s_gq // pb)
    gq = n_a * pb
    starts = ([pb * j for j in range(n_a)] + [s_gq + pb * j for j in range(b_width // pb)]
              + [s_gm + pb * j for j in range(2 * d_model // pb)])
    offs = {"cq": 0, "ckv": q_lora, "krope": s_kr, "gq": gq, "gk": gq + gla_key,
            "gv": gq + 2 * gla_key, "gr": gq + 2 * gla_key + gla_val, "gm": gq + b_width}
    return offs, starts, pb, s_glr


def _in_proj_kernel(a_ref, wt_ref, o_ref):
    o_ref[...] = lax.dot_general(a_ref[...], wt_ref[...], NT_DIMS, preferred_element_type=F32)


def _in_proj(a, w_t, starts, bn, *, bm, name):
    m, kdim = a.shape
    bm = _block(m, bm)

    sub = 16
    assert all(s % sub == 0 for s in starts)

    def row_start(j):
        r = jnp.int32(starts[-1] // sub)
        for idx in range(len(starts) - 2, -1, -1):
            r = jnp.where(j == idx, jnp.int32(starts[idx] // sub), r)
        return pl.multiple_of(r * sub, sub)

    return pl.pallas_call(
        _in_proj_kernel,
        grid=(m // bm, len(starts)),
        in_specs=[pl.BlockSpec((bm, kdim), lambda i, j: (i, 0)),
                  pl.BlockSpec((pl.Element(bn), pl.Element(kdim)),
                               lambda i, j: (row_start(j), 0))],
        out_specs=pl.BlockSpec((bm, bn), lambda i, j: (i, j)),
        out_shape=jax.ShapeDtypeStruct((m, bn * len(starts)), F32),
        compiler_params=_params(2),
        name=name,
    )(a, w_t)


def _pack_uq(w_uq):
    q_lora = w_uq.shape[0]
    w = w_uq.reshape(q_lora, MLA_HEADS, MLA_NOPE + MLA_ROPE)
    rope = w[:, :, MLA_NOPE:]
    half = MLA_ROPE // 2
    rot = jnp.concatenate([-rope[:, :, half:], rope[:, :, :half]], axis=2)
    return jnp.concatenate([w, rot], axis=2).reshape(q_lora, MLA_HEADS * MLA_QK_PAD).astype(BF16)


def _pack_ukv(w_ukv):
    kv_lora = w_ukv.shape[0]
    w = w_ukv.reshape(kv_lora, MLA_HEADS, MLA_NOPE + MLA_V)
    wk = w[:, :, :MLA_NOPE].reshape(kv_lora, MLA_HEADS * MLA_NOPE)
    wv = w[:, :, MLA_NOPE:].reshape(kv_lora, MLA_HEADS * MLA_V)
    return wk.astype(BF16), wv.astype(BF16)


def _pack_gate(w2, gb):
    r, key = w2.shape[1], w2.shape[2]
    z = jnp.zeros((r, key), w2.dtype)
    top = jnp.concatenate([w2[0], z], axis=1)
    bot = jnp.concatenate([z, w2[1]], axis=1)
    tail = jnp.zeros((LANES - 2 * r, 2 * key), w2.dtype)
    return jnp.concatenate([top, bot, tail], axis=0).astype(BF16), gb.reshape(1, 2 * key)


def kernel(x_prompt, x_sample, mem_prompt, mem_sample, ln_in_g, ln_in_b, w_in, b_merge, mla_q_norm, w_uq, mla_kv_norm, w_ukv, gla_gate_w2, gla_gate_b, gla_norm, w_branch_mla, w_branch_gla, w_mix_out, ln1_g, ln1_b, xa_wq, xa_wkv, xa_wo, ln2_g, ln2_b, mlp_w1, mlp_w2, ln3_g, ln3_b):
    assert w_in.shape[0] == DEPTH
    ba, sa, d = x_prompt.shape
    bb, sb, _ = x_sample.shape
    n_mem = mem_prompt.shape[1]
    ta, tb = ba * sa, bb * sb
    groups = ((0, ba, sa), (ta, bb, sb))
    q_lora, kv_lora = mla_q_norm.shape[1], mla_kv_norm.shape[1]
    assert ta % sb == 0 and ta % 1024 == 0 and sa % 1024 == 0 and sb % 1024 == 0

    offs, in_starts, in_bn, glr_start = _in_proj_layout(d, q_lora, kv_lora)
    w_in_t = jnp.swapaxes(w_in[0], 0, 1).astype(BF16)
    wq_p = _pack_uq(w_uq[0])
    wk_p, wv_p = _pack_ukv(w_ukv[0])
    w2cat, gbcat = _pack_gate(gla_gate_w2[0], gla_gate_b[0])
    cos, sin = _rope_tables(max(sa, sb))

    h, h_b = _ln_in(x_prompt.reshape(ta, d), x_sample.reshape(tb, d), ln_in_g, ln_in_b)

    p = _in_proj(h_b, w_in_t, in_starts, in_bn, bm=1024, name="in_proj")
    p_glr = _in_proj(h_b, w_in_t, [glr_start], LANES, bm=2048, name="in_proj_gate")
    q = _q_proj(p, offs["cq"], q_lora, mla_q_norm[0], wq_p, cos, sin, groups, 1024)
    k, v = _kv_proj(p, offs["ckv"], kv_lora, offs["krope"], mla_kv_norm[0], wk_p, wv_p,
                    cos, sin, groups, 512)
    o_mla = _mla_attention(q, k, v, groups)
    gla_f = _gla_direction(p, p_glr, offs, w2cat, gbcat, groups, reverse=False)
    o_gla = _gla_direction(p, p_glr, offs, w2cat, gbcat, groups, reverse=True, fwd=gla_f,
                           norm_g=gla_norm[0])
    merged = _branch_merge(o_mla, o_gla, w_branch_mla[0].astype(BF16),
                           w_branch_gla[0].astype(BF16), p, offs["gm"], b_merge[0])
    h, h_b = _matmul_ln(merged, w_mix_out[0].astype(BF16), h, ln1_g[0], ln1_b[0], bm=256,
                        name="mix_out_ln")

    mem = jnp.concatenate([mem_prompt.reshape(ba * n_mem, d), mem_sample.reshape(bb * n_mem, d)])
    xq = _matmul_wres(h_b, xa_wq[0], BF16, name="xa_q")
    xkv = _matmul_wres(mem.astype(BF16), xa_wkv[0], BF16, bm=2048, bn=512, name="xa_kv")
    o_x = _cross_attention(xq, xkv, groups, n_mem)
    h, h_b = _matmul_ln(o_x, xa_wo[0].astype(BF16), h, ln2_g[0], ln2_b[0], bm=256,
                        name="xa_o_ln")

    u = _matmul_wres(h_b, mlp_w1[0], BF16, relu2=True, name="mlp_up")
    z = _matmul_res(u, mlp_w2[0].astype(BF16), h, name="mlp_down")
    y_a = _ln_rows(z, ln3_g[0], ln3_b[0], row_off=0, rows=ta)
    y_b = _ln_rows(z, ln3_g[0], ln3_b[0], row_off=ta, rows=tb)
    return y_a.reshape(ba, sa, d), y_b.reshape(bb, sb, d)
```

```python
import functools

import numpy as np
import jax
import jax.numpy as jnp
from jax import lax
from jax.experimental import pallas as pl
from jax.experimental.pallas import tpu as pltpu

MLA_HEADS = 16
MLA_NOPE = 128
MLA_ROPE = 64
MLA_V = 128
ROPE_THETA = 10000.0
GLA_HEADS = 4
GLA_DK = 256
GLA_DV = 512
GLA_GATE_RANK = 16
GLA_GATE_NORM = 16.0
GLA_CHUNK = 64
XA_HEADS = 4
LN_EPS = 1e-5
RMS_EPS = 1e-6
DEPTH = 1
DN_ALPHA = (2.0 * DEPTH) ** 0.25

LANES = 128
MLA_QK_PAD = 2 * LANES
MLA_V_PAD = 2 * LANES
MLA_Q_ROWS = 256
VMEM_LIMIT_BYTES = 56 * 2**20
VMEM_LIMIT_LARGE_BYTES = 63 * 2**20

BF16 = jnp.bfloat16
F32 = jnp.float32
NT_DIMS = (((1,), (1,)), ((), ()))
TN_DIMS = (((0,), (0,)), ((), ()))


def _params(n_grid, vmem_limit_bytes=VMEM_LIMIT_BYTES):
    return pltpu.CompilerParams(dimension_semantics=("arbitrary",) * n_grid,
                                vmem_limit_bytes=vmem_limit_bytes)


def _block(n, pref):
    b = min(n, pref)
    while n % b:
        b //= 2
    return b


def _layer_norm(x, g, b):
    mu = jnp.mean(x, -1, keepdims=True)
    xc = x - mu
    var = jnp.mean(xc * xc, -1, keepdims=True)
    return xc * lax.rsqrt(var + LN_EPS) * g + b


def _rms_norm(x, g):
    return x * lax.rsqrt(jnp.mean(x * x, -1, keepdims=True) + RMS_EPS) * g


def _sigmoid(x):
    return 1.0 / (1.0 + jnp.exp(-x))


def _rope(x, cos, sin):
    return x * cos + pltpu.roll(x, MLA_ROPE, 1) * sin


def _rope_rotate(x, cos, sin):
    half = MLA_ROPE // 2
    lane = lax.broadcasted_iota(jnp.int32, x.shape, 1)
    rot = jnp.where(lane < half, -pltpu.roll(x, LANES - half, 1), pltpu.roll(x, half, 1))
    return x * cos + rot * sin


def _ln_in_kernel(xa_ref, xb_ref, g_ref, b_ref, h_ref, hb_ref, *, n_a):
    def emit(x_ref):
        y = _layer_norm(x_ref[...], g_ref[...], b_ref[...])
        h_ref[...] = y
        hb_ref[...] = y.astype(BF16)

    @pl.when(pl.program_id(0) < n_a)
    def _():
        emit(xa_ref)

    @pl.when(pl.program_id(0) >= n_a)
    def _():
        emit(xb_ref)


def _ln_in(xa, xb, g, b):
    ta, d = xa.shape
    tb = xb.shape[0]
    bm = _block(np.gcd(ta, tb), 256)
    n_a, n_b = ta // bm, tb // bm
    row = pl.BlockSpec((1, d), lambda i: (0, 0))
    out = pl.BlockSpec((bm, d), lambda i: (i, 0))
    return pl.pallas_call(
        functools.partial(_ln_in_kernel, n_a=n_a),
        grid=(n_a + n_b,),
        in_specs=[pl.BlockSpec((bm, d), lambda i: (jnp.minimum(i, n_a - 1), 0)),
                  pl.BlockSpec((bm, d), lambda i: (jnp.maximum(i - n_a, 0), 0)),
                  row, row],
        out_specs=[out, out],
        out_shape=[jax.ShapeDtypeStruct((ta + tb, d), F32),
                   jax.ShapeDtypeStruct((ta + tb, d), BF16)],
        compiler_params=_params(1),
        name="ln_in",
    )(xa, xb, g.reshape(1, d), b.reshape(1, d))


def _ln_rows_kernel(z_ref, g_ref, b_ref, o_ref):
    o_ref[...] = _layer_norm(z_ref[...], g_ref[...], b_ref[...])


def _ln_rows(z, g, b, *, row_off, rows):
    d = z.shape[1]
    bm = _block(np.gcd(rows, row_off) if row_off else rows, 256)
    off = row_off // bm
    row = pl.BlockSpec((1, d), lambda i: (0, 0))
    return pl.pallas_call(
        _ln_rows_kernel,
        grid=(rows // bm,),
        in_specs=[pl.BlockSpec((bm, d), lambda i: (i + off, 0)), row, row],
        out_specs=pl.BlockSpec((bm, d), lambda i: (i, 0)),
        out_shape=jax.ShapeDtypeStruct((rows, d), F32),
        compiler_params=_params(1),
        name="ln_out",
    )(z, g.reshape(1, d), b.reshape(1, d))


def _matmul_res_kernel(a_ref, b_ref, h_ref, o_ref, *scratch, nk):
    part = jnp.dot(a_ref[...], b_ref[...], preferred_element_type=F32)

    def finish(acc):
        o_ref[...] = DN_ALPHA * h_ref[...] + acc

    if nk == 1:
        finish(part)
        return
    acc_ref, = scratch
    k = pl.program_id(2)

    @pl.when(k == 0)
    def _():
        acc_ref[...] = part

    @pl.when(jnp.logical_and(k > 0, k < nk - 1))
    def _():
        acc_ref[...] += part

    @pl.when(k == nk - 1)
    def _():
        finish(acc_ref[...] + part)


def _matmul_res(a, b, h, *, bm=1024, bn=1024, bk=4096, name="matmul_res"):
    m, kdim = a.shape
    n = b.shape[1]
    bm, bn, bk = _block(m, bm), _block(n, bn), _block(kdim, bk)
    nk = kdim // bk
    tile = pl.BlockSpec((bm, bn), lambda i, j, k: (i, j))
    return pl.pallas_call(
        functools.partial(_matmul_res_kernel, nk=nk),
        grid=(m // bm, n // bn, nk),
        in_specs=[pl.BlockSpec((bm, bk), lambda i, j, k: (i, k)),
                  pl.BlockSpec((bk, bn), lambda i, j, k: (k, j)),
                  tile],
        out_specs=tile,
        out_shape=jax.ShapeDtypeStruct((m, n), F32),
        scratch_shapes=[pltpu.VMEM((bm, bn), F32)] if nk > 1 else [],
        compiler_params=_params(3, VMEM_LIMIT_LARGE_BYTES),
        name=name,
    )(a, b, h)


def _matmul_kernel(a_ref, b_ref, o_ref, *, relu2):
    acc = jnp.dot(a_ref[...], b_ref[...], preferred_element_type=F32)
    if relu2:
        acc = jnp.square(jnp.maximum(acc, 0.0))
    o_ref[...] = acc.astype(o_ref.dtype)


def _matmul(a, b, out_dtype, *, bm=1024, bn=1024, relu2=False, name="matmul"):
    m, kdim = a.shape
    n = b.shape[1]
    bm, bn = _block(m, bm), _block(n, bn)
    return pl.pallas_call(
        functools.partial(_matmul_kernel, relu2=relu2),
        grid=(m // bm, n // bn),
        in_specs=[pl.BlockSpec((bm, kdim), lambda i, j: (i, 0)),
                  pl.BlockSpec((kdim, bn), lambda i, j: (0, j))],
        out_specs=pl.BlockSpec((bm, bn), lambda i, j: (i, j)),
        out_shape=jax.ShapeDtypeStruct((m, n), out_dtype),
        compiler_params=_params(2),
        name=name,
    )(a, b)


def _matmul_wres_kernel(a_ref, w_ref, o_ref, wb_ref, *, relu2):
    @pl.when(pl.program_id(1) == 0)
    def _():
        wb_ref[...] = w_ref[...].astype(BF16)

    acc = jnp.dot(a_ref[...], wb_ref[...], preferred_element_type=F32)
    if relu2:
        acc = jnp.square(jnp.maximum(acc, 0.0))
    o_ref[...] = acc.astype(o_ref.dtype)


def _matmul_wres(a, w, out_dtype, *, bm=512, bn=1024, relu2=False, name="matmul_wres"):
    m, kdim = a.shape
    n = w.shape[1]
    bm, bn = _block(m, bm), _block(n, bn)
    return pl.pallas_call(
        functools.partial(_matmul_wres_kernel, relu2=relu2),
        grid=(n // bn, m // bm),
        in_specs=[pl.BlockSpec((bm, kdim), lambda j, i: (i, 0)),
                  pl.BlockSpec((kdim, bn), lambda j, i: (0, j))],
        out_specs=pl.BlockSpec((bm, bn), lambda j, i: (i, j)),
        out_shape=jax.ShapeDtypeStruct((m, n), out_dtype),
        scratch_shapes=[pltpu.VMEM((kdim, bn), BF16)],
        compiler_params=_params(2),
        name=name,
    )(a, w)


def _matmul_ln_kernel(a_ref, w_ref, h_ref, g_ref, b_ref, o_ref, ob_ref, *, chains):
    rows_per = a_ref.shape[0] // chains
    for c in range(chains):
        rows = pl.ds(c * rows_per, rows_per)
        y = jnp.dot(a_ref[rows, :], w_ref[...], preferred_element_type=F32)
        z = _layer_norm(DN_ALPHA * h_ref[rows, :] + y, g_ref[...], b_ref[...])
        o_ref[rows, :] = z
        ob_ref[rows, :] = z.astype(BF16)


def _matmul_ln(a, w, h, g, b, *, bm, name):
    m, kdim = a.shape
    n = w.shape[1]
    bm = _block(m, bm)
    row = pl.BlockSpec((1, n), lambda i: (0, 0))
    tile = pl.BlockSpec((bm, n), lambda i: (i, 0))
    return pl.pallas_call(
        functools.partial(_matmul_ln_kernel, chains=2 if bm % 256 == 0 else 1),
        grid=(m // bm,),
        in_specs=[pl.BlockSpec((bm, kdim), lambda i: (i, 0)),
                  pl.BlockSpec((kdim, n), lambda i: (0, 0), pipeline_mode=pl.Buffered(1)),
                  tile, row, row],
        out_specs=[tile, tile],
        out_shape=[jax.ShapeDtypeStruct((m, n), F32), jax.ShapeDtypeStruct((m, n), BF16)],
        compiler_params=_params(1, VMEM_LIMIT_LARGE_BYTES),
        name=name,
    )(a, w, h, g.reshape(1, n), b.reshape(1, n))


def _rope_tables(seq_len):
    half = MLA_ROPE // 2
    inv = 1.0 / (ROPE_THETA ** (jnp.arange(0, MLA_ROPE, 2, dtype=F32) / MLA_ROPE))
    ang = jnp.arange(seq_len, dtype=F32)[:, None] * inv[None, :]
    zero = jnp.zeros((seq_len, LANES - 2 * half), F32)
    cos = jnp.concatenate([jnp.cos(ang), jnp.cos(ang), zero], axis=1)
    sin = jnp.concatenate([jnp.sin(ang), jnp.sin(ang), zero], axis=1)
    return cos, sin


def _pos_block_map(groups, bm):
    (_, _, s_a), (off_b, _, s_b) = groups
    n_a = off_b // bm

    def index(i):
        return jnp.where(i < n_a, i % (s_a // bm), (i - n_a) % (s_b // bm))
    return index


def _qproj_kernel(c_ref, g_ref, w_ref, cos_ref, sin_ref, o_ref, xn_ref, *, heads, scale):
    @pl.when(pl.program_id(1) == 0)
    def _():
        xn_ref[...] = _rms_norm(c_ref[...], g_ref[...]).astype(BF16)

    r = jnp.dot(xn_ref[...], w_ref[...], preferred_element_type=F32)
    cos, sin = cos_ref[...], sin_ref[...]
    for h in range(heads):
        lo = h * MLA_QK_PAD
        o_ref[:, lo:lo + MLA_NOPE] = (r[:, lo:lo + MLA_NOPE] * scale).astype(BF16)
        o_ref[:, lo + MLA_NOPE:lo + MLA_QK_PAD] = (
            _rope(r[:, lo + MLA_NOPE:lo + MLA_QK_PAD], cos, sin) * scale).astype(BF16)


def _q_proj(p, c_off, q_lora, g, wq, cos, sin, groups, bm):
    t = p.shape[0]
    heads = min(8, MLA_HEADS)
    bn = heads * MLA_QK_PAD
    pos = _pos_block_map(groups, bm)
    scale = (MLA_NOPE + MLA_ROPE) ** -0.5 * np.log2(np.e)
    return pl.pallas_call(
        functools.partial(_qproj_kernel, heads=heads, scale=scale),
        grid=(t // bm, MLA_HEADS // heads),
        in_specs=[pl.BlockSpec((bm, q_lora), lambda i, j: (i, c_off // q_lora)),
                  pl.BlockSpec((1, q_lora), lambda i, j: (0, 0)),
                  pl.BlockSpec((q_lora, bn), lambda i, j: (0, j)),
                  pl.BlockSpec((bm, LANES), lambda i, j: (pos(i), 0)),
                  pl.BlockSpec((bm, LANES), lambda i, j: (pos(i), 0))],
        out_specs=pl.BlockSpec((bm, bn), lambda i, j: (i, j)),
        out_shape=jax.ShapeDtypeStruct((t, MLA_HEADS * MLA_QK_PAD), BF16),
        scratch_shapes=[pltpu.VMEM((bm, q_lora), BF16)],
        compiler_params=_params(2),
        name="mla_q_proj",
    )(p, g.reshape(1, q_lora), wq, cos, sin)


def _kvproj_kernel(c_ref, kr_ref, g_ref, wk_ref, wv_ref, cos_ref, sin_ref, k_ref, v_ref):
    xn = _rms_norm(c_ref[...], g_ref[...]).astype(BF16)
    kn = jnp.dot(xn, wk_ref[...], preferred_element_type=F32)
    vn = jnp.dot(xn, wv_ref[...], preferred_element_type=F32)
    k_rope = _rope_rotate(kr_ref[...], cos_ref[...], sin_ref[...]).astype(BF16)
    ones = jnp.ones((k_ref.shape[0], MLA_V_PAD - MLA_V), BF16)
    for h in range(MLA_HEADS):
        lo = h * MLA_QK_PAD
        k_ref[:, lo:lo + MLA_NOPE] = kn[:, h * MLA_NOPE:(h + 1) * MLA_NOPE].astype(BF16)
        k_ref[:, lo + MLA_NOPE:lo + MLA_QK_PAD] = k_rope
        lo = h * MLA_V_PAD
        v_ref[:, lo:lo + MLA_V] = vn[:, h * MLA_V:(h + 1) * MLA_V].astype(BF16)
        v_ref[:, lo + MLA_V:lo + MLA_V_PAD] = ones


def _kv_proj(p, c_off, kv_lora, kr_off, g, wk, wv, cos, sin, groups, bm):
    t = p.shape[0]
    pos = _pos_block_map(groups, bm)
    nk, nv = MLA_HEADS * MLA_QK_PAD, MLA_HEADS * MLA_V
    return pl.pallas_call(
        _kvproj_kernel,
        grid=(t // bm,),
        in_specs=[pl.BlockSpec((bm, kv_lora), lambda i: (i, c_off // kv_lora)),
                  pl.BlockSpec((bm, LANES), lambda i: (i, kr_off // LANES)),
                  pl.BlockSpec((1, kv_lora), lambda i: (0, 0)),
                  pl.BlockSpec((kv_lora, MLA_HEADS * MLA_NOPE), lambda i: (0, 0)),
                  pl.BlockSpec((kv_lora, nv), lambda i: (0, 0)),
                  pl.BlockSpec((bm, LANES), lambda i: (pos(i), 0)),
                  pl.BlockSpec((bm, LANES), lambda i: (pos(i), 0))],
        out_specs=[pl.BlockSpec((bm, nk), lambda i: (i, 0)),
                   pl.BlockSpec((bm, MLA_HEADS * MLA_V_PAD), lambda i: (i, 0))],
        out_shape=[jax.ShapeDtypeStruct((t, nk), BF16),
                   jax.ShapeDtypeStruct((t, MLA_HEADS * MLA_V_PAD), BF16)],
        compiler_params=_params(1),
        name="mla_kv_proj",
    )(p, p, g.reshape(1, kv_lora), wk, wv, cos, sin)


def _group_call(kernel, *, grid, in_specs, out_spec, out, args, scratch_shapes=(), name):
    def body(*refs):
        n_in = len(in_specs)
        kernel(*refs[:n_in], *refs[n_in + 1:])

    return pl.pallas_call(
        body,
        grid=grid,
        in_specs=list(in_specs) + [pl.BlockSpec(memory_space=pl.ANY)],
        out_specs=out_spec,
        out_shape=jax.ShapeDtypeStruct(out.shape, out.dtype),
        input_output_aliases={len(in_specs): 0},
        scratch_shapes=list(scratch_shapes),
        compiler_params=_params(len(grid)),
        name=name,
    )(*args, out)


def _softmax_pv(s, v):
    m = jnp.max(s, -1, keepdims=True)
    e = jnp.exp(s - m)
    l = jnp.sum(e, -1, keepdims=True)
    return jnp.dot(e.astype(BF16), v, preferred_element_type=F32) / l


def _mla_attn_kernel(q_ref, k_ref, v_ref, o_ref, *, chains):
    for c in range(chains):
        rows = pl.ds(c * MLA_Q_ROWS, MLA_Q_ROWS)
        s = lax.dot_general(q_ref[rows, :], k_ref[...], NT_DIMS, preferred_element_type=F32)
        e = jnp.exp2(s - jnp.max(s, -1, keepdims=True)).astype(BF16)
        acc = jnp.dot(e, v_ref[...], preferred_element_type=F32)
        o_ref[rows, :] = (acc[:, :MLA_V] / acc[:, MLA_V:]).astype(o_ref.dtype)


def _mla_attention(q, k, v, groups):
    t = q.shape[0]
    out = jnp.zeros((t, MLA_HEADS * MLA_V), BF16)
    for row_off, batch, seq in groups:
        bq = _block(seq, 8 * MLA_Q_ROWS)
        nq = seq // bq
        q_row = lambda b, h, i, nq=nq, o=row_off // bq: (o + b * nq + i, h)
        kv_row = lambda b, h, i, o=row_off // seq: (o + b, h)
        out = _group_call(
            functools.partial(_mla_attn_kernel, chains=bq // MLA_Q_ROWS),
            grid=(batch, MLA_HEADS, nq),
            in_specs=[pl.BlockSpec((bq, MLA_QK_PAD), q_row),
                      pl.BlockSpec((seq, MLA_QK_PAD), kv_row),
                      pl.BlockSpec((seq, MLA_V_PAD), kv_row)],
            out_spec=pl.BlockSpec((bq, MLA_V), q_row),
            out=out, args=(q, k, v), name="mla_attention")
    return out


GLA_SUPER = 4 * GLA_CHUNK


def _cumsum_rows(tri_b, x):
    hi = x.astype(BF16)
    r1 = x - hi.astype(F32)
    mid = r1.astype(BF16)
    lo = (r1 - mid.astype(F32)).astype(BF16)
    dot = lambda t: jnp.dot(tri_b, t, preferred_element_type=F32)
    return dot(hi) + dot(mid) + dot(lo)


def _gla_kernel(*refs, reverse, is_first, final):
    if final:
        (q_ref, k_ref, v_ref, lr_ref, w2_ref, gb_ref, fwd_ref, gr_ref, ng_ref,
         o_ref, st_ref) = refs
    else:
        q_ref, k_ref, v_ref, lr_ref, w2_ref, gb_ref, o_ref, st_ref = refs
    c, n_rows = GLA_CHUNK, GLA_SUPER
    nb = n_rows // c

    @pl.when(is_first(pl.program_id(0)))
    def _():
        st_ref[...] = jnp.zeros_like(st_ref)

    gate = jnp.dot(lr_ref[...].astype(BF16), w2_ref[...], preferred_element_type=F32) + gb_ref[...]
    log_a_all = ((jnp.minimum(gate, 0.0) - jnp.log1p(jnp.exp(-jnp.abs(gate))))
                 * (np.log2(np.e) / GLA_GATE_NORM))

    row = lax.broadcasted_iota(jnp.int32, (n_rows, n_rows), 0)
    col = lax.broadcasted_iota(jnp.int32, (n_rows, n_rows), 1)
    before = (col > row) if reverse else (col < row)
    upto = (col >= row) if reverse else (col <= row)
    same_chunk = (row // c) == (col // c)
    same_pair = (row // (2 * c)) == (col // (2 * c))
    m_chunk = jnp.logical_and(same_chunk, upto)
    m_pair = jnp.logical_and(jnp.logical_and(same_pair, jnp.logical_not(same_chunk)), before)
    m_cross = jnp.logical_and(jnp.logical_not(same_pair), before)

    tri_b = upto.astype(BF16)
    order = list(reversed(range(nb))) if reverse else list(range(nb))
    per_chunk = lambda r: jnp.concatenate(
        [jnp.broadcast_to(r[b], (c, GLA_DK)) for b in range(nb)], axis=0)
    nt = lambda a, b: lax.dot_general(a, b, NT_DIMS, preferred_element_type=F32)

    for h in range(GLA_HEADS):
        dk = slice(h * GLA_DK, (h + 1) * GLA_DK)
        dv = slice(h * GLA_DV, (h + 1) * GLA_DV)
        cum = _cumsum_rows(tri_b, log_a_all[:, dk])
        r_start, r_end = {}, {}
        r_prev = jnp.zeros((1, GLA_DK), F32)
        for b in order:
            e = b * c if reverse else b * c + c - 1
            r_start[b] = r_prev
            r_end[b] = cum[e:e + 1]
            r_prev = r_end[b]
        total = r_prev
        mid = r_end[order[nb // 2 - 1]]
        cumloc = cum - per_chunk(r_start)
        to_end = per_chunk(r_end) - cum

        q = q_ref[:, dk] * (GLA_DK ** -0.5)
        k = k_ref[:, dk]
        v = v_ref[:, dv].astype(BF16)
        q_loc = (q * jnp.exp2(cumloc)).astype(BF16)
        k_loc = (k * jnp.exp2(-cumloc)).astype(BF16)
        k_end = (k * jnp.exp2(to_end)).astype(BF16)
        q_mid = (q * jnp.exp2(jnp.minimum(cum - mid, 0.0))).astype(BF16)
        k_mid = (k * jnp.exp2(jnp.minimum(mid - cum, 0.0))).astype(BF16)
        q_all = (q * jnp.exp2(cum)).astype(BF16)
        k_all = (k * jnp.exp2(total - cum)).astype(BF16)

        attn = jnp.where(m_chunk, nt(q_loc, k_loc),
                         jnp.where(m_pair, nt(q_loc, k_end),
                                   jnp.where(m_cross, nt(q_mid, k_mid), 0.0))).astype(BF16)
        state = st_ref[h]
        o = jnp.dot(attn, v, preferred_element_type=F32) + nt(q_all, state.astype(BF16))
        st_ref[h] = state * jnp.exp2(total) + lax.dot_general(
            v, k_all, TN_DIMS, preferred_element_type=F32)
        if final:
            tot = fwd_ref[:, dv] + o
            gr = gr_ref[:, dv]
            o_ref[:, dv] = (_rms_norm(tot, ng_ref[...]) * (gr * _sigmoid(gr))).astype(o_ref.dtype)
        else:
            o_ref[:, dv] = o


def _gla_direction(p, p_glr, offs, w2cat, gbcat, groups, *, reverse, fwd=None, norm_g=None):
    t = p.shape[0]
    final = fwd is not None
    lb = GLA_SUPER
    (_, _, seq_a), (off_b, _, seq_b) = groups
    n_a, ns_a, ns_b = off_b // lb, seq_a // lb, seq_b // lb
    assert seq_a % lb == 0 and seq_b % lb == 0
    d = 1 if reverse else 0

    def local(u):
        in_a = u < n_a
        return jnp.where(in_a, u % ns_a, (u - n_a) % ns_b), jnp.where(in_a, ns_a, ns_b)

    def rows(u):
        n, ns = local(u)
        return (u - n) + (ns - 1 - n) if reverse else u

    is_first = lambda u: local(u)[0] == 0
    key, val = GLA_HEADS * GLA_DK, GLA_HEADS * GLA_DV
    assert offs["gq"] % key == 0 and offs["gk"] % key == 0
    assert offs["gv"] % val == 0 and offs["gr"] % val == 0
    col = lambda off, width: (lambda u: (rows(u), off // width))
    in_specs = [pl.BlockSpec((lb, key), col(offs["gq"], key)),
                pl.BlockSpec((lb, key), col(offs["gk"], key)),
                pl.BlockSpec((lb, val), col(offs["gv"], val)),
                pl.BlockSpec((lb, LANES), col(0, LANES)),
                pl.BlockSpec((LANES, key), lambda u: (0, d)),
                pl.BlockSpec((1, key), lambda u: (0, d))]
    args = [p, p, p, p_glr, w2cat, gbcat]
    if final:
        in_specs += [pl.BlockSpec((lb, val), col(0, val)),
                     pl.BlockSpec((lb, val), col(offs["gr"], val)),
                     pl.BlockSpec((1, GLA_DV), lambda u: (0, 0))]
        args += [fwd, p, norm_g.reshape(1, GLA_DV)]
    return pl.pallas_call(
        functools.partial(_gla_kernel, reverse=reverse, is_first=is_first, final=final),
        grid=(t // lb,),
        in_specs=in_specs,
        out_specs=pl.BlockSpec((lb, val), col(0, val)),
        out_shape=jax.ShapeDtypeStruct((t, val), BF16 if final else F32),
        scratch_shapes=[pltpu.VMEM((GLA_HEADS, GLA_DV, GLA_DK), F32)],
        compiler_params=_params(1),
        name="gla_bwd" if reverse else "gla_fwd",
    )(*args)


def _merge_kernel(a1_ref, a2_ref, w1_ref, w2_ref, g1_ref, g2_ref, b1_ref, b2_ref, o_ref):
    y1 = jnp.dot(a1_ref[...], w1_ref[...], preferred_element_type=F32)
    y2 = jnp.dot(a2_ref[...], w2_ref[...], preferred_element_type=F32)
    s1 = _sigmoid(g1_ref[...] + b1_ref[...])
    s2 = _sigmoid(g2_ref[...] + b2_ref[...])
    o_ref[...] = (s1 * y1 + s2 * y2).astype(o_ref.dtype)


def _branch_merge(o_mla, o_gla, w_mla, w_gla, p, gm_off, b_merge):
    t, k1 = o_mla.shape
    k2 = o_gla.shape[1]
    d = w_mla.shape[1]
    bm, bn = _block(t, 1024), _block(d, 512)
    g0, nd = gm_off // bn, d // bn
    return pl.pallas_call(
        _merge_kernel,
        grid=(t // bm, nd),
        in_specs=[pl.BlockSpec((bm, k1), lambda i, j: (i, 0)),
                  pl.BlockSpec((bm, k2), lambda i, j: (i, 0)),
                  pl.BlockSpec((k1, bn), lambda i, j: (0, j)),
                  pl.BlockSpec((k2, bn), lambda i, j: (0, j)),
                  pl.BlockSpec((bm, bn), lambda i, j: (i, g0 + j)),
                  pl.BlockSpec((bm, bn), lambda i, j: (i, g0 + nd + j)),
                  pl.BlockSpec((1, bn), lambda i, j: (0, j)),
                  pl.BlockSpec((1, bn), lambda i, j: (0, nd + j))],
        out_specs=pl.BlockSpec((bm, bn), lambda i, j: (i, j)),
        out_shape=jax.ShapeDtypeStruct((t, d), BF16),
        compiler_params=_params(2),
        name="branch_merge",
    )(o_mla, o_gla, w_mla, w_gla, p, p, b_merge.reshape(1, 2 * d), b_merge.reshape(1, 2 * d))


def _xattn_kernel(q_ref, k_ref, v_ref, o_ref, *, scale):
    hd = q_ref.shape[1] // XA_HEADS
    for h in range(XA_HEADS):
        cols = slice(h * hd, (h + 1) * hd)
        s = lax.dot_general(q_ref[:, cols], k_ref[:, cols], NT_DIMS,
                            preferred_element_type=F32) * scale
        o_ref[:, cols] = _softmax_pv(s, v_ref[:, cols]).astype(o_ref.dtype)


def _cross_attention(q, kv, groups, n_mem):
    t, d = q.shape
    (_, batch_a, seq_a), (off_b, _, seq_b) = groups
    bq = _block(np.gcd(seq_a, seq_b), 512)
    n_a, per_a, per_b = off_b // bq, seq_a // bq, seq_b // bq
    mem_row = lambda u: jnp.where(u < n_a, u // per_a, batch_a + (u - n_a) // per_b)
    return pl.pallas_call(
        functools.partial(_xattn_kernel, scale=(d // XA_HEADS) ** -0.5),
        grid=(t // bq,),
        in_specs=[pl.BlockSpec((bq, d), lambda u: (u, 0)),
                  pl.BlockSpec((n_mem, d), lambda u: (mem_row(u), 0)),
                  pl.BlockSpec((n_mem, d), lambda u: (mem_row(u), 1))],
        out_specs=pl.BlockSpec((bq, d), lambda u: (u, 0)),
        out_shape=jax.ShapeDtypeStruct((t, d), BF16),
        compiler_params=_params(1),
        name="cross_attention",
    )(q, kv, kv)


def _in_proj_layout(d_model, q_lora, kv_lora):
    gla_key, gla_val = GLA_HEADS * GLA_DK, GLA_HEADS * GLA_DV
    b_width = 2 * gla_key + 2 * gla_val
    s_kr = q_lora + kv_lora
    s_gq = s_kr + MLA_ROPE
    s_glr = s_gq + b_width
    s_gm = s_glr + 2 * GLA_GATE_RANK
    pb = 1024 if b_width % 1024 == 0 and (2 * d_model) % 1024 == 0 else 512
    assert b_width % pb == 0 and (2 * d_model) % pb == 0
    assert s_kr % LANES == 0 and q_lora % kv_lora == 0 and 2 * GLA_GATE_RANK <= LANES
    assert s_gq % 8 == 0 and s_glr % 8 == 0 and s_gm % 8 == 0
    n_a = -(-s_gq // pb)
    gq = n_a * pb
    starts = ([pb * j for j in range(n_a)] + [s_gq + pb * j for j in range(b_width // pb)]
              + [s_gm + pb * j for j in range(2 * d_model // pb)])
    offs = {"cq": 0, "ckv": q_lora, "krope": s_kr, "gq": gq, "gk": gq + gla_key,
            "gv": gq + 2 * gla_key, "gr": gq + 2 * gla_key + gla_val, "gm": gq + b_width}
    return offs, starts, pb, s_glr


BF16_SUBLANES = 16


def _in_proj_kernel(*refs, n_cast, cast_steps, n_cols):
    a_ref, wt_ref = refs[:2]
    cast_in = refs[2:2 + n_cast]
    o_ref = refs[2 + n_cast]
    cast_out = refs[3 + n_cast:]
    o_ref[...] = lax.dot_general(a_ref[...], wt_ref[...], NT_DIMS, preferred_element_type=F32)
    step = pl.program_id(0) * n_cols + pl.program_id(1)
    for src, dst, active in zip(cast_in, cast_out, cast_steps):
        @pl.when(step < active)
        def _(src=src, dst=dst):
            dst[...] = src[...].astype(BF16)


def _in_proj(a, w_t, starts, bn, *, bm, name, cast_weights=()):
    m, kdim = a.shape
    bm = _block(m, bm)
    n_cols = len(starts)
    n_steps = (m // bm) * n_cols

    assert all(s % BF16_SUBLANES == 0 for s in starts)

    def row_start(j):
        r = jnp.int32(starts[-1] // BF16_SUBLANES)
        for idx in range(n_cols - 2, -1, -1):
            r = jnp.where(j == idx, jnp.int32(starts[idx] // BF16_SUBLANES), r)
        return pl.multiple_of(r * BF16_SUBLANES, BF16_SUBLANES)

    cast_specs, cast_steps = [], []
    for w in cast_weights:
        rows = BF16_SUBLANES
        while w.shape[0] % rows or w.shape[0] // rows > n_steps:
            rows += BF16_SUBLANES
        active = w.shape[0] // rows
        cast_steps.append(active)
        cast_specs.append(pl.BlockSpec(
            (rows, w.shape[1]),
            lambda i, j, active=active: (jnp.minimum(i * n_cols + j, active - 1), 0)))

    out = pl.pallas_call(
        functools.partial(_in_proj_kernel, n_cast=len(cast_weights),
                          cast_steps=tuple(cast_steps), n_cols=n_cols),
        grid=(m // bm, n_cols),
        in_specs=[pl.BlockSpec((bm, kdim), lambda i, j: (i, 0)),
                  pl.BlockSpec((pl.Element(bn), pl.Element(kdim)),
                               lambda i, j: (row_start(j), 0))] + cast_specs,
        out_specs=[pl.BlockSpec((bm, bn), lambda i, j: (i, j))] + cast_specs,
        out_shape=[jax.ShapeDtypeStruct((m, bn * n_cols), F32)]
        + [jax.ShapeDtypeStruct(w.shape, BF16) for w in cast_weights],
        compiler_params=_params(2, VMEM_LIMIT_LARGE_BYTES if cast_weights else VMEM_LIMIT_BYTES),
        name=name,
    )(a, w_t, *cast_weights)
    return out[0], tuple(out[1:])


def _pack_uq(w_uq):
    q_lora = w_uq.shape[0]
    w = w_uq.reshape(q_lora, MLA_HEADS, MLA_NOPE + MLA_ROPE)
    rope = w[:, :, MLA_NOPE:]
    half = MLA_ROPE // 2
    rot = jnp.concatenate([-rope[:, :, half:], rope[:, :, :half]], axis=2)
    return jnp.concatenate([w, rot], axis=2).reshape(q_lora, MLA_HEADS * MLA_QK_PAD).astype(BF16)


def _pack_ukv(w_ukv):
    kv_lora = w_ukv.shape[0]
    w = w_ukv.reshape(kv_lora, MLA_HEADS, MLA_NOPE + MLA_V)
    wk = w[:, :, :MLA_NOPE].reshape(kv_lora, MLA_HEADS * MLA_NOPE)
    wv = w[:, :, MLA_NOPE:].reshape(kv_lora, MLA_HEADS * MLA_V)
    return wk.astype(BF16), wv.astype(BF16)


def _pack_gate(w2, gb):
    r, key = w2.shape[1], w2.shape[2]
    z = jnp.zeros((r, key), w2.dtype)
    top = jnp.concatenate([w2[0], z], axis=1)
    bot = jnp.concatenate([z, w2[1]], axis=1)
    tail = jnp.zeros((LANES - 2 * r, 2 * key), w2.dtype)
    return jnp.concatenate([top, bot, tail], axis=0).astype(BF16), gb.reshape(1, 2 * key)


def kernel(x_prompt, x_sample, mem_prompt, mem_sample, ln_in_g, ln_in_b, w_in, b_merge, mla_q_norm, w_uq, mla_kv_norm, w_ukv, gla_gate_w2, gla_gate_b, gla_norm, w_branch_mla, w_branch_gla, w_mix_out, ln1_g, ln1_b, xa_wq, xa_wkv, xa_wo, ln2_g, ln2_b, mlp_w1, mlp_w2, ln3_g, ln3_b):
    assert w_in.shape[0] == DEPTH
    ba, sa, d = x_prompt.shape
    bb, sb, _ = x_sample.shape
    n_mem = mem_prompt.shape[1]
    ta, tb = ba * sa, bb * sb
    groups = ((0, ba, sa), (ta, bb, sb))
    q_lora, kv_lora = mla_q_norm.shape[1], mla_kv_norm.shape[1]
    assert ta % sb == 0 and ta % 1024 == 0 and sa % 1024 == 0 and sb % 1024 == 0

    offs, in_starts, in_bn, glr_start = _in_proj_layout(d, q_lora, kv_lora)
    w_in_t = jnp.swapaxes(w_in[0], 0, 1).astype(BF16)
    wq_p = _pack_uq(w_uq[0])
    wk_p, wv_p = _pack_ukv(w_ukv[0])
    w2cat, gbcat = _pack_gate(gla_gate_w2[0], gla_gate_b[0])
    cos, sin = _rope_tables(max(sa, sb))

    h, h_b = _ln_in(x_prompt.reshape(ta, d), x_sample.reshape(tb, d), ln_in_g, ln_in_b)

    later_weights = (w_branch_mla[0], w_branch_gla[0], w_mix_out[0], xa_wq[0], xa_wo[0],
                     mlp_w1[0], mlp_w2[0])
    p, (w_br_mla_b, w_br_gla_b, w_mix_b, xa_wq_b, xa_wo_b, mlp_w1_b, mlp_w2_b) = _in_proj(
        h_b, w_in_t, in_starts, in_bn, bm=1024, name="in_proj", cast_weights=later_weights)
    p_glr, _ = _in_proj(h_b, w_in_t, [glr_start], LANES, bm=2048, name="in_proj_gate")
    q = _q_proj(p, offs["cq"], q_lora, mla_q_norm[0], wq_p, cos, sin, groups, 1024)
    k, v = _kv_proj(p, offs["ckv"], kv_lora, offs["krope"], mla_kv_norm[0], wk_p, wv_p,
                    cos, sin, groups, 512)
    o_mla = _mla_attention(q, k, v, groups)
    gla_f = _gla_direction(p, p_glr, offs, w2cat, gbcat, groups, reverse=False)
    o_gla = _gla_direction(p, p_glr, offs, w2cat, gbcat, groups, reverse=True, fwd=gla_f,
                           norm_g=gla_norm[0])
    merged = _branch_merge(o_mla, o_gla, w_br_mla_b, w_br_gla_b, p, offs["gm"], b_merge[0])
    h, h_b = _matmul_ln(merged, w_mix_b, h, ln1_g[0], ln1_b[0], bm=256, name="mix_out_ln")

    mem = jnp.concatenate([mem_prompt.reshape(ba * n_mem, d), mem_sample.reshape(bb * n_mem, d)])
    xq = _matmul(h_b, xa_wq_b, BF16, name="xa_q")
    xkv = _matmul_wres(mem.astype(BF16), xa_wkv[0], BF16, bm=2048, bn=512, name="xa_kv")
    o_x = _cross_attention(xq, xkv, groups, n_mem)
    h, h_b = _matmul_ln(o_x, xa_wo_b, h, ln2_g[0], ln2_b[0], bm=256, name="xa_o_ln")

    u = _matmul(h_b, mlp_w1_b, BF16, relu2=True, name="mlp_up")
    z = _matmul_res(u, mlp_w2_b, h, name="mlp_down")
    y_a = _ln_rows(z, ln3_g[0], ln3_b[0], row_off=0, rows=ta)
    y_b = _ln_rows(z, ln3_g[0], ln3_b[0], row_off=ta, rows=tb)
    return y_a.reshape(ba, sa, d), y_b.reshape(bb, sb, d)
```

```python
import functools

import numpy as np
import jax
import jax.numpy as jnp
from jax import lax
from jax.experimental import pallas as pl
from jax.experimental.pallas import tpu as pltpu

MLA_HEADS = 16
MLA_NOPE = 128
MLA_ROPE = 64
MLA_V = 128
ROPE_THETA = 10000.0
GLA_HEADS = 4
GLA_DK = 256
GLA_DV = 512
GLA_GATE_RANK = 16
GLA_GATE_NORM = 16.0
GLA_CHUNK = 64
XA_HEADS = 4
LN_EPS = 1e-5
RMS_EPS = 1e-6
DEPTH = 1
DN_ALPHA = (2.0 * DEPTH) ** 0.25

LANES = 128
MLA_QK_PAD = 2 * LANES
MLA_V_PAD = 2 * LANES
MLA_Q_ROWS = 256
VMEM_LIMIT_BYTES = 56 * 2**20
VMEM_LIMIT_LARGE_BYTES = 63 * 2**20

BF16 = jnp.bfloat16
F32 = jnp.float32
NT_DIMS = (((1,), (1,)), ((), ()))
TN_DIMS = (((0,), (0,)), ((), ()))


def _params(n_grid, vmem_limit_bytes=VMEM_LIMIT_BYTES):
    return pltpu.CompilerParams(dimension_semantics=("arbitrary",) * n_grid,
                                vmem_limit_bytes=vmem_limit_bytes)


def _block(n, pref):
    b = min(n, pref)
    while n % b:
        b //= 2
    return b


def _layer_norm(x, g, b):
    mu = jnp.mean(x, -1, keepdims=True)
    xc = x - mu
    var = jnp.mean(xc * xc, -1, keepdims=True)
    return xc * lax.rsqrt(var + LN_EPS) * g + b


def _rms_norm(x, g):
    return x * lax.rsqrt(jnp.mean(x * x, -1, keepdims=True) + RMS_EPS) * g


def _sigmoid(x):
    return 1.0 / (1.0 + jnp.exp(-x))


def _rope(x, cos, sin):
    return x * cos + pltpu.roll(x, MLA_ROPE, 1) * sin


def _rope_rotate(x, cos, sin):
    half = MLA_ROPE // 2
    lane = lax.broadcasted_iota(jnp.int32, x.shape, 1)
    rot = jnp.where(lane < half, -pltpu.roll(x, LANES - half, 1), pltpu.roll(x, half, 1))
    return x * cos + rot * sin


def _ln_in_kernel(xa_ref, xb_ref, g_ref, b_ref, h_ref, hb_ref, *, n_a):
    def emit(x_ref):
        y = _layer_norm(x_ref[...], g_ref[...], b_ref[...])
        h_ref[...] = y
        hb_ref[...] = y.astype(BF16)

    @pl.when(pl.program_id(0) < n_a)
    def _():
        emit(xa_ref)

    @pl.when(pl.program_id(0) >= n_a)
    def _():
        emit(xb_ref)


def _ln_in(xa, xb, g, b):
    ta, d = xa.shape
    tb = xb.shape[0]
    bm = _block(np.gcd(ta, tb), 256)
    n_a, n_b = ta // bm, tb // bm
    row = pl.BlockSpec((1, d), lambda i: (0, 0))
    out = pl.BlockSpec((bm, d), lambda i: (i, 0))
    return pl.pallas_call(
        functools.partial(_ln_in_kernel, n_a=n_a),
        grid=(n_a + n_b,),
        in_specs=[pl.BlockSpec((bm, d), lambda i: (jnp.minimum(i, n_a - 1), 0)),
                  pl.BlockSpec((bm, d), lambda i: (jnp.maximum(i - n_a, 0), 0)),
                  row, row],
        out_specs=[out, out],
        out_shape=[jax.ShapeDtypeStruct((ta + tb, d), F32),
                   jax.ShapeDtypeStruct((ta + tb, d), BF16)],
        compiler_params=_params(1),
        name="ln_in",
    )(xa, xb, g.reshape(1, d), b.reshape(1, d))


def _ln_rows_kernel(z_ref, g_ref, b_ref, o_ref):
    o_ref[...] = _layer_norm(z_ref[...], g_ref[...], b_ref[...])


def _ln_rows(z, g, b, *, row_off, rows):
    d = z.shape[1]
    bm = _block(np.gcd(rows, row_off) if row_off else rows, 256)
    off = row_off // bm
    row = pl.BlockSpec((1, d), lambda i: (0, 0))
    return pl.pallas_call(
        _ln_rows_kernel,
        grid=(rows // bm,),
        in_specs=[pl.BlockSpec((bm, d), lambda i: (i + off, 0)), row, row],
        out_specs=pl.BlockSpec((bm, d), lambda i: (i, 0)),
        out_shape=jax.ShapeDtypeStruct((rows, d), F32),
        compiler_params=_params(1),
        name="ln_out",
    )(z, g.reshape(1, d), b.reshape(1, d))


def _matmul_res_kernel(a_ref, b_ref, h_ref, o_ref, *scratch, nk):
    part = jnp.dot(a_ref[...], b_ref[...], preferred_element_type=F32)

    def finish(acc):
        o_ref[...] = DN_ALPHA * h_ref[...] + acc

    if nk == 1:
        finish(part)
        return
    acc_ref, = scratch
    k = pl.program_id(2)

    @pl.when(k == 0)
    def _():
        acc_ref[...] = part

    @pl.when(jnp.logical_and(k > 0, k < nk - 1))
    def _():
        acc_ref[...] += part

    @pl.when(k == nk - 1)
    def _():
        finish(acc_ref[...] + part)


def _matmul_res(a, b, h, *, bm=1024, bn=1024, bk=4096, name="matmul_res"):
    m, kdim = a.shape
    n = b.shape[1]
    bm, bn, bk = _block(m, bm), _block(n, bn), _block(kdim, bk)
    nk = kdim // bk
    tile = pl.BlockSpec((bm, bn), lambda i, j, k: (i, j))
    return pl.pallas_call(
        functools.partial(_matmul_res_kernel, nk=nk),
        grid=(m // bm, n // bn, nk),
        in_specs=[pl.BlockSpec((bm, bk), lambda i, j, k: (i, k)),
                  pl.BlockSpec((bk, bn), lambda i, j, k: (k, j)),
                  tile],
        out_specs=tile,
        out_shape=jax.ShapeDtypeStruct((m, n), F32),
        scratch_shapes=[pltpu.VMEM((bm, bn), F32)] if nk > 1 else [],
        compiler_params=_params(3, VMEM_LIMIT_LARGE_BYTES),
        name=name,
    )(a, b, h)


def _matmul_kernel(a_ref, b_ref, o_ref, *, relu2):
    acc = jnp.dot(a_ref[...], b_ref[...], preferred_element_type=F32)
    if relu2:
        acc = jnp.square(jnp.maximum(acc, 0.0))
    o_ref[...] = acc.astype(o_ref.dtype)


def _matmul(a, b, out_dtype, *, bm=1024, bn=1024, relu2=False, name="matmul"):
    m, kdim = a.shape
    n = b.shape[1]
    bm, bn = _block(m, bm), _block(n, bn)
    return pl.pallas_call(
        functools.partial(_matmul_kernel, relu2=relu2),
        grid=(m // bm, n // bn),
        in_specs=[pl.BlockSpec((bm, kdim), lambda i, j: (i, 0)),
                  pl.BlockSpec((kdim, bn), lambda i, j: (0, j))],
        out_specs=pl.BlockSpec((bm, bn), lambda i, j: (i, j)),
        out_shape=jax.ShapeDtypeStruct((m, n), out_dtype),
        compiler_params=_params(2),
        name=name,
    )(a, b)


def _matmul_ln_kernel(a_ref, w_ref, h_ref, g_ref, b_ref, o_ref, ob_ref, y0_ref, y1_ref):
    step = pl.program_id(0)

    @pl.when(step == 0)
    def _():
        y1_ref[...] = jnp.zeros_like(y1_ref)

    def run(prev_ref, cur_ref):
        z = _layer_norm(DN_ALPHA * h_ref[...] + prev_ref[...], g_ref[...], b_ref[...])
        o_ref[...] = z
        ob_ref[...] = z.astype(BF16)
        cur_ref[...] = jnp.dot(a_ref[...], w_ref[...], preferred_element_type=F32)

    @pl.when(step % 2 == 0)
    def _():
        run(y1_ref, y0_ref)

    @pl.when(step % 2 == 1)
    def _():
        run(y0_ref, y1_ref)


def _matmul_ln(a, w, h, g, b, *, bm, name):
    m, kdim = a.shape
    n = w.shape[1]
    bm = _block(m, bm)
    nt = m // bm
    row = pl.BlockSpec((1, n), lambda s: (0, 0))
    lagged = pl.BlockSpec((bm, n), lambda s: (jnp.maximum(s - 1, 0), 0))
    return pl.pallas_call(
        _matmul_ln_kernel,
        grid=(nt + 1,),
        in_specs=[pl.BlockSpec((bm, kdim), lambda s: (jnp.minimum(s, nt - 1), 0)),
                  pl.BlockSpec((kdim, n), lambda s: (0, 0), pipeline_mode=pl.Buffered(1)),
                  lagged, row, row],
        out_specs=[lagged, lagged],
        out_shape=[jax.ShapeDtypeStruct((m, n), F32), jax.ShapeDtypeStruct((m, n), BF16)],
        scratch_shapes=[pltpu.VMEM((bm, n), F32), pltpu.VMEM((bm, n), F32)],
        compiler_params=_params(1),
        name=name,
    )(a, w, h, g.reshape(1, n), b.reshape(1, n))


def _rope_tables(seq_len):
    half = MLA_ROPE // 2
    inv = 1.0 / (ROPE_THETA ** (jnp.arange(0, MLA_ROPE, 2, dtype=F32) / MLA_ROPE))
    ang = jnp.arange(seq_len, dtype=F32)[:, None] * inv[None, :]
    zero = jnp.zeros((seq_len, LANES - 2 * half), F32)
    cos = jnp.concatenate([jnp.cos(ang), jnp.cos(ang), zero], axis=1)
    sin = jnp.concatenate([jnp.sin(ang), jnp.sin(ang), zero], axis=1)
    return cos, sin


def _pos_block_map(groups, bm):
    (_, _, s_a), (off_b, _, s_b) = groups
    n_a = off_b // bm

    def index(i):
        return jnp.where(i < n_a, i % (s_a // bm), (i - n_a) % (s_b // bm))
    return index


def _qproj_kernel(c_ref, g_ref, w_ref, cos_ref, sin_ref, o_ref, xn_ref, *, heads, scale):
    @pl.when(pl.program_id(1) == 0)
    def _():
        xn_ref[...] = _rms_norm(c_ref[...], g_ref[...]).astype(BF16)

    r = jnp.dot(xn_ref[...], w_ref[...], preferred_element_type=F32)
    cos, sin = cos_ref[...], sin_ref[...]
    for h in range(heads):
        lo = h * MLA_QK_PAD
        o_ref[:, lo:lo + MLA_NOPE] = (r[:, lo:lo + MLA_NOPE] * scale).astype(BF16)
        o_ref[:, lo + MLA_NOPE:lo + MLA_QK_PAD] = (
            _rope(r[:, lo + MLA_NOPE:lo + MLA_QK_PAD], cos, sin) * scale).astype(BF16)


def _q_proj(p, c_off, q_lora, g, wq, cos, sin, groups, bm):
    t = p.shape[0]
    heads = min(8, MLA_HEADS)
    bn = heads * MLA_QK_PAD
    pos = _pos_block_map(groups, bm)
    scale = (MLA_NOPE + MLA_ROPE) ** -0.5 * np.log2(np.e)
    return pl.pallas_call(
        functools.partial(_qproj_kernel, heads=heads, scale=scale),
        grid=(t // bm, MLA_HEADS // heads),
        in_specs=[pl.BlockSpec((bm, q_lora), lambda i, j: (i, c_off // q_lora)),
                  pl.BlockSpec((1, q_lora), lambda i, j: (0, 0)),
                  pl.BlockSpec((q_lora, bn), lambda i, j: (0, j)),
                  pl.BlockSpec((bm, LANES), lambda i, j: (pos(i), 0)),
                  pl.BlockSpec((bm, LANES), lambda i, j: (pos(i), 0))],
        out_specs=pl.BlockSpec((bm, bn), lambda i, j: (i, j)),
        out_shape=jax.ShapeDtypeStruct((t, MLA_HEADS * MLA_QK_PAD), BF16),
        scratch_shapes=[pltpu.VMEM((bm, q_lora), BF16)],
        compiler_params=_params(2),
        name="mla_q_proj",
    )(p, g.reshape(1, q_lora), wq, cos, sin)


def _kvproj_kernel(c_ref, kr_ref, g_ref, wk_ref, wv_ref, cos_ref, sin_ref, k_ref, v_ref):
    xn = _rms_norm(c_ref[...], g_ref[...]).astype(BF16)
    kn = jnp.dot(xn, wk_ref[...], preferred_element_type=F32)
    vn = jnp.dot(xn, wv_ref[...], preferred_element_type=F32)
    k_rope = _rope_rotate(kr_ref[...], cos_ref[...], sin_ref[...]).astype(BF16)
    ones = jnp.ones((k_ref.shape[0], MLA_V_PAD - MLA_V), BF16)
    for h in range(MLA_HEADS):
        lo = h * MLA_QK_PAD
        k_ref[:, lo:lo + MLA_NOPE] = kn[:, h * MLA_NOPE:(h + 1) * MLA_NOPE].astype(BF16)
        k_ref[:, lo + MLA_NOPE:lo + MLA_QK_PAD] = k_rope
        lo = h * MLA_V_PAD
        v_ref[:, lo:lo + MLA_V] = vn[:, h * MLA_V:(h + 1) * MLA_V].astype(BF16)
        v_ref[:, lo + MLA_V:lo + MLA_V_PAD] = ones


def _kv_proj(p, c_off, kv_lora, kr_off, g, wk, wv, cos, sin, groups, bm):
    t = p.shape[0]
    pos = _pos_block_map(groups, bm)
    nk, nv = MLA_HEADS * MLA_QK_PAD, MLA_HEADS * MLA_V
    return pl.pallas_call(
        _kvproj_kernel,
        grid=(t // bm,),
        in_specs=[pl.BlockSpec((bm, kv_lora), lambda i: (i, c_off // kv_lora)),
                  pl.BlockSpec((bm, LANES), lambda i: (i, kr_off // LANES)),
                  pl.BlockSpec((1, kv_lora), lambda i: (0, 0)),
                  pl.BlockSpec((kv_lora, MLA_HEADS * MLA_NOPE), lambda i: (0, 0)),
                  pl.BlockSpec((kv_lora, nv), lambda i: (0, 0)),
                  pl.BlockSpec((bm, LANES), lambda i: (pos(i), 0)),
                  pl.BlockSpec((bm, LANES), lambda i: (pos(i), 0))],
        out_specs=[pl.BlockSpec((bm, nk), lambda i: (i, 0)),
                   pl.BlockSpec((bm, MLA_HEADS * MLA_V_PAD), lambda i: (i, 0))],
        out_shape=[jax.ShapeDtypeStruct((t, nk), BF16),
                   jax.ShapeDtypeStruct((t, MLA_HEADS * MLA_V_PAD), BF16)],
        compiler_params=_params(1),
        name="mla_kv_proj",
    )(p, p, g.reshape(1, kv_lora), wk, wv, cos, sin)


def _group_call(kernel, *, grid, in_specs, out_spec, out, args, more_out_specs=(),
                more_out_shapes=(), scratch_shapes=(), name):
    def body(*refs):
        n_in = len(in_specs)
        kernel(*refs[:n_in], *refs[n_in + 1:])

    return pl.pallas_call(
        body,
        grid=grid,
        in_specs=list(in_specs) + [pl.BlockSpec(memory_space=pl.ANY)],
        out_specs=[out_spec, *more_out_specs],
        out_shape=[jax.ShapeDtypeStruct(out.shape, out.dtype), *more_out_shapes],
        input_output_aliases={len(in_specs): 0},
        scratch_shapes=list(scratch_shapes),
        compiler_params=_params(len(grid)),
        name=name,
    )(*args, out)


def _softmax_pv(s, v):
    m = jnp.max(s, -1, keepdims=True)
    e = jnp.exp(s - m)
    l = jnp.sum(e, -1, keepdims=True)
    return jnp.dot(e.astype(BF16), v, preferred_element_type=F32) / l


def _mla_attn_chain(q_ref, k_ref, v_ref, o_ref, c):
    rows = pl.ds(c * MLA_Q_ROWS, MLA_Q_ROWS)
    s = lax.dot_general(q_ref[rows, :], k_ref[...], NT_DIMS, preferred_element_type=F32)
    e = jnp.exp2(s - jnp.max(s, -1, keepdims=True)).astype(BF16)
    acc = jnp.dot(e, v_ref[...], preferred_element_type=F32)
    o_ref[rows, :] = (acc[:, :MLA_V] / acc[:, MLA_V:]).astype(o_ref.dtype)


def _attn_gla_kernel(*refs, chains, n_gla_in, gla_kernel):
    q_ref, k_ref, v_ref = refs[:3]
    gla_in = refs[3:3 + n_gla_in]
    o_attn_ref, o_gla_ref, st_ref = refs[3 + n_gla_in:]
    gla_parts = gla_kernel(*gla_in, o_gla_ref, st_ref)
    next(gla_parts)
    every = max(1, chains // GLA_HEADS)
    for c in range(chains):
        if c % every == 0:
            next(gla_parts, None)
        _mla_attn_chain(q_ref, k_ref, v_ref, o_attn_ref, c)
    for _ in gla_parts:
        pass


def _mixer_attention(q, k, v, groups, gla_pass):
    t = q.shape[0]
    n_steps = t // GLA_SUPER
    out = jnp.zeros((t, MLA_HEADS * MLA_V), BF16)
    gla_out = None
    for g, (row_off, batch, seq) in enumerate(groups):
        bq = batch * seq * MLA_HEADS // n_steps
        assert seq % bq == 0 and bq % MLA_Q_ROWS == 0 and bq // MLA_Q_ROWS <= 8
        nq = seq // bq
        step = lambda b, h, i, nq=nq: (b * MLA_HEADS + h) * nq + i
        q_row = lambda b, h, i, nq=nq, o=row_off // bq: (o + b * nq + i, h)
        kv_row = lambda b, h, i, o=row_off // seq: (o + b, h)
        job = gla_pass(g, step, gla_out)
        out, gla_out = _group_call(
            functools.partial(_attn_gla_kernel, chains=bq // MLA_Q_ROWS,
                              n_gla_in=len(job["in_specs"]), gla_kernel=job["kernel"]),
            grid=(batch, MLA_HEADS, nq),
            in_specs=[pl.BlockSpec((bq, MLA_QK_PAD), q_row),
                      pl.BlockSpec((seq, MLA_QK_PAD), kv_row),
                      pl.BlockSpec((seq, MLA_V_PAD), kv_row)] + job["in_specs"],
            out_spec=pl.BlockSpec((bq, MLA_V), q_row),
            out=out, args=(q, k, v, *job["args"]),
            more_out_specs=[job["out_spec"]], more_out_shapes=[job["out_shape"]],
            scratch_shapes=job["scratch_shapes"], name="mla_attention_gla")
    return out, gla_out


GLA_SUPER = 4 * GLA_CHUNK


def _cumsum_rows(tri_b, x):
    hi = x.astype(BF16)
    r1 = x - hi.astype(F32)
    mid = r1.astype(BF16)
    lo = (r1 - mid.astype(F32)).astype(BF16)
    dot = lambda t: jnp.dot(tri_b, t, preferred_element_type=F32)
    return dot(hi) + dot(mid) + dot(lo)


def _gla_kernel(*refs, reverse, is_first, final, step):
    if final:
        (q_ref, k_ref, v_ref, lr_ref, w2_ref, gb_ref, fwd_ref, gr_ref, ng_ref,
         o_ref, st_ref) = refs
    else:
        q_ref, k_ref, v_ref, lr_ref, w2_ref, gb_ref, o_ref, st_ref = refs
    c, n_rows = GLA_CHUNK, GLA_SUPER
    nb = n_rows // c

    @pl.when(is_first(step(*(pl.program_id(a) for a in range(3)))))
    def _():
        st_ref[...] = jnp.zeros_like(st_ref)

    gate = jnp.dot(lr_ref[...].astype(BF16), w2_ref[...], preferred_element_type=F32) + gb_ref[...]
    log_a_all = ((jnp.minimum(gate, 0.0) - jnp.log1p(jnp.exp(-jnp.abs(gate))))
                 * (np.log2(np.e) / GLA_GATE_NORM))

    row = lax.broadcasted_iota(jnp.int32, (n_rows, n_rows), 0)
    col = lax.broadcasted_iota(jnp.int32, (n_rows, n_rows), 1)
    before = (col > row) if reverse else (col < row)
    upto = (col >= row) if reverse else (col <= row)
    same_chunk = (row // c) == (col // c)
    same_pair = (row // (2 * c)) == (col // (2 * c))
    m_chunk = jnp.logical_and(same_chunk, upto)
    m_pair = jnp.logical_and(jnp.logical_and(same_pair, jnp.logical_not(same_chunk)), before)
    m_cross = jnp.logical_and(jnp.logical_not(same_pair), before)

    tri_b = upto.astype(BF16)
    order = list(reversed(range(nb))) if reverse else list(range(nb))
    per_chunk = lambda r: jnp.concatenate(
        [jnp.broadcast_to(r[b], (c, GLA_DK)) for b in range(nb)], axis=0)
    nt = lambda a, b: lax.dot_general(a, b, NT_DIMS, preferred_element_type=F32)

    yield

    for h in range(GLA_HEADS):
        dk = slice(h * GLA_DK, (h + 1) * GLA_DK)
        dv = slice(h * GLA_DV, (h + 1) * GLA_DV)
        cum = _cumsum_rows(tri_b, log_a_all[:, dk])
        r_start, r_end = {}, {}
        r_prev = jnp.zeros((1, GLA_DK), F32)
        for b in order:
            e = b * c if reverse else b * c + c - 1
            r_start[b] = r_prev
            r_end[b] = cum[e:e + 1]
            r_prev = r_end[b]
        total = r_prev
        mid = r_end[order[nb // 2 - 1]]
        cumloc = cum - per_chunk(r_start)
        to_end = per_chunk(r_end) - cum

        q = q_ref[:, dk] * (GLA_DK ** -0.5)
        k = k_ref[:, dk]
        v = v_ref[:, dv].astype(BF16)
        q_loc = (q * jnp.exp2(cumloc)).astype(BF16)
        k_loc = (k * jnp.exp2(-cumloc)).astype(BF16)
        k_end = (k * jnp.exp2(to_end)).astype(BF16)
        q_mid = (q * jnp.exp2(jnp.minimum(cum - mid, 0.0))).astype(BF16)
        k_mid = (k * jnp.exp2(jnp.minimum(mid - cum, 0.0))).astype(BF16)
        q_all = (q * jnp.exp2(cum)).astype(BF16)
        k_all = (k * jnp.exp2(total - cum)).astype(BF16)

        attn = jnp.where(m_chunk, nt(q_loc, k_loc),
                         jnp.where(m_pair, nt(q_loc, k_end),
                                   jnp.where(m_cross, nt(q_mid, k_mid), 0.0))).astype(BF16)
        state = st_ref[h]
        o = jnp.dot(attn, v, preferred_element_type=F32) + nt(q_all, state.astype(BF16))
        st_ref[h] = state * jnp.exp2(total) + lax.dot_general(
            v, k_all, TN_DIMS, preferred_element_type=F32)
        if final:
            tot = fwd_ref[:, dv] + o
            gr = gr_ref[:, dv]
            o_ref[:, dv] = (_rms_norm(tot, ng_ref[...]) * (gr * _sigmoid(gr))).astype(o_ref.dtype)
        else:
            o_ref[:, dv] = o
        yield


def _gla_job(p, p_glr, offs, w2cat, gbcat, groups, step, *, reverse, fwd=None, norm_g=None):
    t = p.shape[0]
    final = fwd is not None
    lb = GLA_SUPER
    (_, _, seq_a), (off_b, _, seq_b) = groups
    n_a, ns_a, ns_b = off_b // lb, seq_a // lb, seq_b // lb
    assert seq_a % lb == 0 and seq_b % lb == 0
    d = 1 if reverse else 0

    def local(u):
        in_a = u < n_a
        return jnp.where(in_a, u % ns_a, (u - n_a) % ns_b), jnp.where(in_a, ns_a, ns_b)

    def rows(u):
        n, ns = local(u)
        return (u - n) + (ns - 1 - n) if reverse else u

    is_first = lambda u: local(u)[0] == 0
    key, val = GLA_HEADS * GLA_DK, GLA_HEADS * GLA_DV
    assert offs["gq"] % key == 0 and offs["gk"] % key == 0
    assert offs["gv"] % val == 0 and offs["gr"] % val == 0
    col = lambda off, width: (lambda *ids: (rows(step(*ids)), off // width))
    in_specs = [pl.BlockSpec((lb, key), col(offs["gq"], key)),
                pl.BlockSpec((lb, key), col(offs["gk"], key)),
                pl.BlockSpec((lb, val), col(offs["gv"], val)),
                pl.BlockSpec((lb, LANES), col(0, LANES)),
                pl.BlockSpec((LANES, key), lambda *ids: (0, d)),
                pl.BlockSpec((1, key), lambda *ids: (0, d))]
    args = [p, p, p, p_glr, w2cat, gbcat]
    if final:
        in_specs += [pl.BlockSpec((lb, val), col(0, val)),
                     pl.BlockSpec((lb, val), col(offs["gr"], val)),
                     pl.BlockSpec((1, GLA_DV), lambda *ids: (0, 0))]
        args += [fwd, p, norm_g.reshape(1, GLA_DV)]
    return dict(
        kernel=functools.partial(_gla_kernel, reverse=reverse, is_first=is_first, final=final,
                                 step=step),
        in_specs=in_specs, args=args,
        out_spec=pl.BlockSpec((lb, val), col(0, val)),
        out_shape=jax.ShapeDtypeStruct((t, val), BF16 if final else F32),
        scratch_shapes=[pltpu.VMEM((GLA_HEADS, GLA_DV, GLA_DK), F32)])


def _merge_kernel(a1_ref, a2_ref, w1_ref, w2_ref, g1_ref, g2_ref, b1_ref, b2_ref, o_ref):
    y1 = jnp.dot(a1_ref[...], w1_ref[...], preferred_element_type=F32)
    y2 = jnp.dot(a2_ref[...], w2_ref[...], preferred_element_type=F32)
    s1 = _sigmoid(g1_ref[...] + b1_ref[...])
    s2 = _sigmoid(g2_ref[...] + b2_ref[...])
    o_ref[...] = (s1 * y1 + s2 * y2).astype(o_ref.dtype)


def _branch_merge(o_mla, o_gla, w_mla, w_gla, p, gm_off, b_merge):
    t, k1 = o_mla.shape
    k2 = o_gla.shape[1]
    d = w_mla.shape[1]
    bm, bn = _block(t, 1024), _block(d, 512)
    g0, nd = gm_off // bn, d // bn
    return pl.pallas_call(
        _merge_kernel,
        grid=(t // bm, nd),
        in_specs=[pl.BlockSpec((bm, k1), lambda i, j: (i, 0)),
                  pl.BlockSpec((bm, k2), lambda i, j: (i, 0)),
                  pl.BlockSpec((k1, bn), lambda i, j: (0, j)),
                  pl.BlockSpec((k2, bn), lambda i, j: (0, j)),
                  pl.BlockSpec((bm, bn), lambda i, j: (i, g0 + j)),
                  pl.BlockSpec((bm, bn), lambda i, j: (i, g0 + nd + j)),
                  pl.BlockSpec((1, bn), lambda i, j: (0, j)),
                  pl.BlockSpec((1, bn), lambda i, j: (0, nd + j))],
        out_specs=pl.BlockSpec((bm, bn), lambda i, j: (i, j)),
        out_shape=jax.ShapeDtypeStruct((t, d), BF16),
        compiler_params=_params(2),
        name="branch_merge",
    )(o_mla, o_gla, w_mla, w_gla, p, p, b_merge.reshape(1, 2 * d), b_merge.reshape(1, 2 * d))


def _xattn_kernel(q_ref, k_ref, v_ref, o_ref, *, scale):
    hd = q_ref.shape[1] // XA_HEADS
    for h in range(XA_HEADS):
        cols = slice(h * hd, (h + 1) * hd)
        s = lax.dot_general(q_ref[:, cols], k_ref[:, cols], NT_DIMS,
                            preferred_element_type=F32) * scale
        o_ref[:, cols] = _softmax_pv(s, v_ref[:, cols]).astype(o_ref.dtype)


def _cross_attention(q, kv, groups, n_mem):
    t, d = q.shape
    (_, batch_a, seq_a), (off_b, _, seq_b) = groups
    bq = _block(np.gcd(seq_a, seq_b), 512)
    n_a, per_a, per_b = off_b // bq, seq_a // bq, seq_b // bq
    mem_row = lambda u: jnp.where(u < n_a, u // per_a, batch_a + (u - n_a) // per_b)
    return pl.pallas_call(
        functools.partial(_xattn_kernel, scale=(d // XA_HEADS) ** -0.5),
        grid=(t // bq,),
        in_specs=[pl.BlockSpec((bq, d), lambda u: (u, 0)),
                  pl.BlockSpec((n_mem, d), lambda u: (mem_row(u), 0)),
                  pl.BlockSpec((n_mem, d), lambda u: (mem_row(u), 1))],
        out_specs=pl.BlockSpec((bq, d), lambda u: (u, 0)),
        out_shape=jax.ShapeDtypeStruct((t, d), BF16),
        compiler_params=_params(1),
        name="cross_attention",
    )(q, kv, kv)


def _in_proj_layout(d_model, q_lora, kv_lora):
    gla_key, gla_val = GLA_HEADS * GLA_DK, GLA_HEADS * GLA_DV
    b_width = 2 * gla_key + 2 * gla_val
    s_kr = q_lora + kv_lora
    s_gq = s_kr + MLA_ROPE
    s_glr = s_gq + b_width
    s_gm = s_glr + 2 * GLA_GATE_RANK
    pb = 1024 if b_width % 1024 == 0 and (2 * d_model) % 1024 == 0 else 512
    assert b_width % pb == 0 and (2 * d_model) % pb == 0
    assert s_kr % LANES == 0 and q_lora % kv_lora == 0 and 2 * GLA_GATE_RANK <= LANES
    assert s_gq % 8 == 0 and s_glr % 8 == 0 and s_gm % 8 == 0
    n_a = -(-s_gq // pb)
    gq = n_a * pb
    starts = ([pb * j for j in range(n_a)] + [s_gq + pb * j for j in range(b_width // pb)]
              + [s_gm + pb * j for j in range(2 * d_model // pb)])
    offs = {"cq": 0, "ckv": q_lora, "krope": s_kr, "gq": gq, "gk": gq + gla_key,
            "gv": gq + 2 * gla_key, "gr": gq + 2 * gla_key + gla_val, "gm": gq + b_width}
    return offs, starts, pb, s_glr


BF16_SUBLANES = 16


def _in_proj_kernel(*refs, n_cast, cast_steps, n_cols):
    a_ref, wt_ref, wg_ref = refs[:3]
    cast_in = refs[3:3 + n_cast]
    o_ref, og_ref = refs[3 + n_cast:5 + n_cast]
    cast_out = refs[5 + n_cast:]
    o_ref[...] = lax.dot_general(a_ref[...], wt_ref[...], NT_DIMS, preferred_element_type=F32)

    @pl.when(pl.program_id(1) == 0)
    def _():
        og_ref[...] = lax.dot_general(a_ref[...], wg_ref[...], NT_DIMS,
                                      preferred_element_type=F32)

    step = pl.program_id(0) * n_cols + pl.program_id(1)
    for src, dst, active in zip(cast_in, cast_out, cast_steps):
        @pl.when(step < active)
        def _(src=src, dst=dst):
            dst[...] = src[...].astype(BF16)


def _in_proj(a, w_t, starts, bn, gate_start, *, bm, name, cast_weights):
    m, kdim = a.shape
    bm = _block(m, bm)
    n_cols = len(starts)
    n_steps = (m // bm) * n_cols

    assert all(s % BF16_SUBLANES == 0 for s in starts) and gate_start % BF16_SUBLANES == 0

    def row_start(j):
        r = jnp.int32(starts[-1] // BF16_SUBLANES)
        for idx in range(n_cols - 2, -1, -1):
            r = jnp.where(j == idx, jnp.int32(starts[idx] // BF16_SUBLANES), r)
        return pl.multiple_of(r * BF16_SUBLANES, BF16_SUBLANES)

    cast_specs, cast_steps = [], []
    for w in cast_weights:
        rows = BF16_SUBLANES
        while w.shape[0] % rows or w.shape[0] // rows > n_steps:
            rows += BF16_SUBLANES
        active = w.shape[0] // rows
        cast_steps.append(active)
        cast_specs.append(pl.BlockSpec(
            (rows, w.shape[1]),
            lambda i, j, active=active: (jnp.minimum(i * n_cols + j, active - 1), 0)))

    out = pl.pallas_call(
        functools.partial(_in_proj_kernel, n_cast=len(cast_weights),
                          cast_steps=tuple(cast_steps), n_cols=n_cols),
        grid=(m // bm, n_cols),
        in_specs=[pl.BlockSpec((bm, kdim), lambda i, j: (i, 0)),
                  pl.BlockSpec((pl.Element(bn), pl.Element(kdim)),
                               lambda i, j: (row_start(j), 0)),
                  pl.BlockSpec((pl.Element(LANES), pl.Element(kdim)),
                               lambda i, j: (gate_start, 0))] + cast_specs,
        out_specs=[pl.BlockSpec((bm, bn), lambda i, j: (i, j)),
                   pl.BlockSpec((bm, LANES), lambda i, j: (i, 0))] + cast_specs,
        out_shape=[jax.ShapeDtypeStruct((m, bn * n_cols), F32),
                   jax.ShapeDtypeStruct((m, LANES), F32)]
        + [jax.ShapeDtypeStruct(w.shape, BF16) for w in cast_weights],
        compiler_params=_params(2, VMEM_LIMIT_LARGE_BYTES),
        name=name,
    )(a, w_t, w_t, *cast_weights)
    return out[0], out[1], tuple(out[2:])


def _pack_uq(w_uq):
    q_lora = w_uq.shape[0]
    w = w_uq.reshape(q_lora, MLA_HEADS, MLA_NOPE + MLA_ROPE)
    rope = w[:, :, MLA_NOPE:]
    half = MLA_ROPE // 2
    rot = jnp.concatenate([-rope[:, :, half:], rope[:, :, :half]], axis=2)
    return jnp.concatenate([w, rot], axis=2).reshape(q_lora, MLA_HEADS * MLA_QK_PAD).astype(BF16)


def _pack_ukv(w_ukv):
    kv_lora = w_ukv.shape[0]
    w = w_ukv.reshape(kv_lora, MLA_HEADS, MLA_NOPE + MLA_V)
    wk = w[:, :, :MLA_NOPE].reshape(kv_lora, MLA_HEADS * MLA_NOPE)
    wv = w[:, :, MLA_NOPE:].reshape(kv_lora, MLA_HEADS * MLA_V)
    return wk.astype(BF16), wv.astype(BF16)


def _pack_gate(w2, gb):
    r, key = w2.shape[1], w2.shape[2]
    z = jnp.zeros((r, key), w2.dtype)
    top = jnp.concatenate([w2[0], z], axis=1)
    bot = jnp.concatenate([z, w2[1]], axis=1)
    tail = jnp.zeros((LANES - 2 * r, 2 * key), w2.dtype)
    return jnp.concatenate([top, bot, tail], axis=0).astype(BF16), gb.reshape(1, 2 * key)


def kernel(x_prompt, x_sample, mem_prompt, mem_sample, ln_in_g, ln_in_b, w_in, b_merge, mla_q_norm, w_uq, mla_kv_norm, w_ukv, gla_gate_w2, gla_gate_b, gla_norm, w_branch_mla, w_branch_gla, w_mix_out, ln1_g, ln1_b, xa_wq, xa_wkv, xa_wo, ln2_g, ln2_b, mlp_w1, mlp_w2, ln3_g, ln3_b):
    assert w_in.shape[0] == DEPTH
    ba, sa, d = x_prompt.shape
    bb, sb, _ = x_sample.shape
    n_mem = mem_prompt.shape[1]
    ta, tb = ba * sa, bb * sb
    groups = ((0, ba, sa), (ta, bb, sb))
    q_lora, kv_lora = mla_q_norm.shape[1], mla_kv_norm.shape[1]
    assert ta % sb == 0 and ta % 1024 == 0 and sa % 1024 == 0 and sb % 1024 == 0

    offs, in_starts, in_bn, glr_start = _in_proj_layout(d, q_lora, kv_lora)
    w_in_t = jnp.swapaxes(w_in[0], 0, 1).astype(BF16)
    wq_p = _pack_uq(w_uq[0])
    wk_p, wv_p = _pack_ukv(w_ukv[0])
    w2cat, gbcat = _pack_gate(gla_gate_w2[0], gla_gate_b[0])
    cos, sin = _rope_tables(max(sa, sb))

    h, h_b = _ln_in(x_prompt.reshape(ta, d), x_sample.reshape(tb, d), ln_in_g, ln_in_b)

    later_weights = (w_branch_mla[0], w_branch_gla[0], w_mix_out[0], xa_wq[0], xa_wkv[0],
                     xa_wo[0], mlp_w1[0], mlp_w2[0])
    (p, p_glr,
     (w_br_mla_b, w_br_gla_b, w_mix_b, xa_wq_b, xa_wkv_b, xa_wo_b, mlp_w1_b, mlp_w2_b)) = _in_proj(
        h_b, w_in_t, in_starts, in_bn, glr_start, bm=1024, name="in_proj",
        cast_weights=later_weights)
    q = _q_proj(p, offs["cq"], q_lora, mla_q_norm[0], wq_p, cos, sin, groups, 1024)
    k, v = _kv_proj(p, offs["ckv"], kv_lora, offs["krope"], mla_kv_norm[0], wk_p, wv_p,
                    cos, sin, groups, 512)
    def gla_pass(g, step, fwd):
        return _gla_job(p, p_glr, offs, w2cat, gbcat, groups, step, reverse=g == 1,
                        fwd=fwd, norm_g=gla_norm[0])

    o_mla, o_gla = _mixer_attention(q, k, v, groups, gla_pass)
    merged = _branch_merge(o_mla, o_gla, w_br_mla_b, w_br_gla_b, p, offs["gm"], b_merge[0])
    h, h_b = _matmul_ln(merged, w_mix_b, h, ln1_g[0], ln1_b[0], bm=128, name="mix_out_ln")

    mem = jnp.concatenate([mem_prompt.reshape(ba * n_mem, d), mem_sample.reshape(bb * n_mem, d)])
    xq = _matmul(h_b, xa_wq_b, BF16, name="xa_q")
    xkv = _matmul(mem.astype(BF16), xa_wkv_b, BF16, bm=2048, name="xa_kv")
    o_x = _cross_attention(xq, xkv, groups, n_mem)
    h, h_b = _matmul_ln(o_x, xa_wo_b, h, ln2_g[0], ln2_b[0], bm=128, name="xa_o_ln")

    u = _matmul(h_b, mlp_w1_b, BF16, relu2=True, name="mlp_up")
    z = _matmul_res(u, mlp_w2_b, h, name="mlp_down")
    y_a = _ln_rows(z, ln3_g[0], ln3_b[0], row_off=0, rows=ta)
    y_b = _ln_rows(z, ln3_g[0], ln3_b[0], row_off=ta, rows=tb)
    return y_a.reshape(ba, sa, d), y_b.reshape(bb, sb, d)
```

```python
import functools

import numpy as np
import jax
import jax.numpy as jnp
from jax import lax
from jax.experimental import pallas as pl
from jax.experimental.pallas import tpu as pltpu

MLA_HEADS = 16
MLA_NOPE = 128
MLA_ROPE = 64
MLA_V = 128
ROPE_THETA = 10000.0
GLA_HEADS = 4
GLA_DK = 256
GLA_DV = 512
GLA_GATE_RANK = 16
GLA_GATE_NORM = 16.0
GLA_CHUNK = 64
XA_HEADS = 4
LN_EPS = 1e-5
RMS_EPS = 1e-6
DEPTH = 1
DN_ALPHA = (2.0 * DEPTH) ** 0.25

LANES = 128
MLA_QK_PAD = 2 * LANES
MLA_V_PAD = 2 * LANES
MLA_Q_ROWS = 256
VMEM_LIMIT_BYTES = 56 * 2**20
VMEM_LIMIT_LARGE_BYTES = 63 * 2**20

BF16 = jnp.bfloat16
F32 = jnp.float32
NT_DIMS = (((1,), (1,)), ((), ()))
TN_DIMS = (((0,), (0,)), ((), ()))


def _params(n_grid, vmem_limit_bytes=VMEM_LIMIT_BYTES):
    return pltpu.CompilerParams(dimension_semantics=("arbitrary",) * n_grid,
                                vmem_limit_bytes=vmem_limit_bytes)


def _block(n, pref):
    b = min(n, pref)
    while n % b:
        b //= 2
    return b


def _layer_norm(x, g, b):
    mu = jnp.mean(x, -1, keepdims=True)
    xc = x - mu
    var = jnp.mean(xc * xc, -1, keepdims=True)
    return xc * lax.rsqrt(var + LN_EPS) * g + b


def _rms_norm(x, g):
    return x * lax.rsqrt(jnp.mean(x * x, -1, keepdims=True) + RMS_EPS) * g


def _sigmoid(x):
    return 1.0 / (1.0 + jnp.exp(-x))


def _rope(x, cos, sin):
    return x * cos + pltpu.roll(x, MLA_ROPE, 1) * sin


def _rope_rotate(x, cos, sin):
    half = MLA_ROPE // 2
    lane = lax.broadcasted_iota(jnp.int32, x.shape, 1)
    rot = jnp.where(lane < half, -pltpu.roll(x, LANES - half, 1), pltpu.roll(x, half, 1))
    return x * cos + rot * sin


def _ln_in_kernel(xa_ref, xb_ref, g_ref, b_ref, h_ref, hb_ref, *, n_a):
    def emit(x_ref):
        y = _layer_norm(x_ref[...], g_ref[...], b_ref[...])
        h_ref[...] = y
        hb_ref[...] = y.astype(BF16)

    @pl.when(pl.program_id(0) < n_a)
    def _():
        emit(xa_ref)

    @pl.when(pl.program_id(0) >= n_a)
    def _():
        emit(xb_ref)


def _ln_in(xa, xb, g, b):
    ta, d = xa.shape
    tb = xb.shape[0]
    bm = _block(np.gcd(ta, tb), 256)
    n_a, n_b = ta // bm, tb // bm
    row = pl.BlockSpec((1, d), lambda i: (0, 0))
    out = pl.BlockSpec((bm, d), lambda i: (i, 0))
    return pl.pallas_call(
        functools.partial(_ln_in_kernel, n_a=n_a),
        grid=(n_a + n_b,),
        in_specs=[pl.BlockSpec((bm, d), lambda i: (jnp.minimum(i, n_a - 1), 0)),
                  pl.BlockSpec((bm, d), lambda i: (jnp.maximum(i - n_a, 0), 0)),
                  row, row],
        out_specs=[out, out],
        out_shape=[jax.ShapeDtypeStruct((ta + tb, d), F32),
                   jax.ShapeDtypeStruct((ta + tb, d), BF16)],
        compiler_params=_params(1),
        name="ln_in",
    )(xa, xb, g.reshape(1, d), b.reshape(1, d))


def _ln_rows_kernel(z_ref, g_ref, b_ref, o_ref):
    o_ref[...] = _layer_norm(z_ref[...], g_ref[...], b_ref[...])


def _ln_rows(z, g, b, *, row_off, rows):
    d = z.shape[1]
    bm = _block(np.gcd(rows, row_off) if row_off else rows, 256)
    off = row_off // bm
    row = pl.BlockSpec((1, d), lambda i: (0, 0))
    return pl.pallas_call(
        _ln_rows_kernel,
        grid=(rows // bm,),
        in_specs=[pl.BlockSpec((bm, d), lambda i: (i + off, 0)), row, row],
        out_specs=pl.BlockSpec((bm, d), lambda i: (i, 0)),
        out_shape=jax.ShapeDtypeStruct((rows, d), F32),
        compiler_params=_params(1),
        name="ln_out",
    )(z, g.reshape(1, d), b.reshape(1, d))


def _matmul_res_kernel(a_ref, b_ref, h_ref, o_ref, *scratch, nk):
    part = jnp.dot(a_ref[...], b_ref[...], preferred_element_type=F32)

    def finish(acc):
        o_ref[...] = DN_ALPHA * h_ref[...] + acc

    if nk == 1:
        finish(part)
        return
    acc_ref, = scratch
    k = pl.program_id(2)

    @pl.when(k == 0)
    def _():
        acc_ref[...] = part

    @pl.when(jnp.logical_and(k > 0, k < nk - 1))
    def _():
        acc_ref[...] += part

    @pl.when(k == nk - 1)
    def _():
        finish(acc_ref[...] + part)


def _matmul_res(a, b, h, *, bm=1024, bn=1024, bk=4096, name="matmul_res"):
    m, kdim = a.shape
    n = b.shape[1]
    bm, bn, bk = _block(m, bm), _block(n, bn), _block(kdim, bk)
    nk = kdim // bk
    tile = pl.BlockSpec((bm, bn), lambda i, j, k: (i, j))
    return pl.pallas_call(
        functools.partial(_matmul_res_kernel, nk=nk),
        grid=(m // bm, n // bn, nk),
        in_specs=[pl.BlockSpec((bm, bk), lambda i, j, k: (i, k)),
                  pl.BlockSpec((bk, bn), lambda i, j, k: (k, j)),
                  tile],
        out_specs=tile,
        out_shape=jax.ShapeDtypeStruct((m, n), F32),
        scratch_shapes=[pltpu.VMEM((bm, bn), F32)] if nk > 1 else [],
        compiler_params=_params(3, VMEM_LIMIT_LARGE_BYTES),
        name=name,
    )(a, b, h)


def _matmul_kernel(a_ref, b_ref, o_ref, *, relu2):
    acc = jnp.dot(a_ref[...], b_ref[...], preferred_element_type=F32)
    if relu2:
        acc = jnp.square(jnp.maximum(acc, 0.0))
    o_ref[...] = acc.astype(o_ref.dtype)


def _matmul(a, b, out_dtype, *, bm=1024, bn=1024, relu2=False, name="matmul"):
    m, kdim = a.shape
    n = b.shape[1]
    bm, bn = _block(m, bm), _block(n, bn)
    return pl.pallas_call(
        functools.partial(_matmul_kernel, relu2=relu2),
        grid=(m // bm, n // bn),
        in_specs=[pl.BlockSpec((bm, kdim), lambda i, j: (i, 0)),
                  pl.BlockSpec((kdim, bn), lambda i, j: (0, j))],
        out_specs=pl.BlockSpec((bm, bn), lambda i, j: (i, j)),
        out_shape=jax.ShapeDtypeStruct((m, n), out_dtype),
        compiler_params=_params(2),
        name=name,
    )(a, b)


def _matmul_ln_kernel(a_ref, w_ref, h_ref, g_ref, b_ref, o_ref, ob_ref, *, chains):
    rows_per = a_ref.shape[0] // chains
    for c in range(chains):
        rows = pl.ds(c * rows_per, rows_per)
        y = jnp.dot(a_ref[rows, :], w_ref[...], preferred_element_type=F32)
        z = _layer_norm(DN_ALPHA * h_ref[rows, :] + y, g_ref[...], b_ref[...])
        o_ref[rows, :] = z
        ob_ref[rows, :] = z.astype(BF16)


def _matmul_ln(a, w, h, g, b, *, bm, name):
    m, kdim = a.shape
    n = w.shape[1]
    bm = _block(m, bm)
    row = pl.BlockSpec((1, n), lambda i: (0, 0))
    tile = pl.BlockSpec((bm, n), lambda i: (i, 0))
    return pl.pallas_call(
        functools.partial(_matmul_ln_kernel, chains=2 if bm % 256 == 0 else 1),
        grid=(m // bm,),
        in_specs=[pl.BlockSpec((bm, kdim), lambda i: (i, 0)),
                  pl.BlockSpec((kdim, n), lambda i: (0, 0), pipeline_mode=pl.Buffered(1)),
                  tile, row, row],
        out_specs=[tile, tile],
        out_shape=[jax.ShapeDtypeStruct((m, n), F32), jax.ShapeDtypeStruct((m, n), BF16)],
        compiler_params=_params(1, VMEM_LIMIT_LARGE_BYTES),
        name=name,
    )(a, w, h, g.reshape(1, n), b.reshape(1, n))


def _rope_tables(seq_len):
    half = MLA_ROPE // 2
    inv = 1.0 / (ROPE_THETA ** (jnp.arange(0, MLA_ROPE, 2, dtype=F32) / MLA_ROPE))
    ang = jnp.arange(seq_len, dtype=F32)[:, None] * inv[None, :]
    zero = jnp.zeros((seq_len, LANES - 2 * half), F32)
    cos = jnp.concatenate([jnp.cos(ang), jnp.cos(ang), zero], axis=1)
    sin = jnp.concatenate([jnp.sin(ang), jnp.sin(ang), zero], axis=1)
    return cos, sin


def _pos_block_map(groups, bm):
    (_, _, s_a), (off_b, _, s_b) = groups
    n_a = off_b // bm

    def index(i):
        return jnp.where(i < n_a, i % (s_a // bm), (i - n_a) % (s_b // bm))
    return index


def _qproj_kernel(c_ref, g_ref, w_ref, cos_ref, sin_ref, o_ref, xn_ref, *, heads, scale):
    @pl.when(pl.program_id(1) == 0)
    def _():
        xn_ref[...] = _rms_norm(c_ref[...], g_ref[...]).astype(BF16)

    r = jnp.dot(xn_ref[...], w_ref[...], preferred_element_type=F32)
    cos, sin = cos_ref[...], sin_ref[...]
    for h in range(heads):
        lo = h * MLA_QK_PAD
        o_ref[:, lo:lo + MLA_NOPE] = (r[:, lo:lo + MLA_NOPE] * scale).astype(BF16)
        o_ref[:, lo + MLA_NOPE:lo + MLA_QK_PAD] = (
            _rope(r[:, lo + MLA_NOPE:lo + MLA_QK_PAD], cos, sin) * scale).astype(BF16)


def _q_proj(p, c_off, q_lora, g, wq, cos, sin, groups, bm):
    t = p.shape[0]
    heads = min(8, MLA_HEADS)
    bn = heads * MLA_QK_PAD
    pos = _pos_block_map(groups, bm)
    scale = (MLA_NOPE + MLA_ROPE) ** -0.5 * np.log2(np.e)
    return pl.pallas_call(
        functools.partial(_qproj_kernel, heads=heads, scale=scale),
        grid=(t // bm, MLA_HEADS // heads),
        in_specs=[pl.BlockSpec((bm, q_lora), lambda i, j: (i, c_off // q_lora)),
                  pl.BlockSpec((1, q_lora), lambda i, j: (0, 0)),
                  pl.BlockSpec((q_lora, bn), lambda i, j: (0, j)),
                  pl.BlockSpec((bm, LANES), lambda i, j: (pos(i), 0)),
                  pl.BlockSpec((bm, LANES), lambda i, j: (pos(i), 0))],
        out_specs=pl.BlockSpec((bm, bn), lambda i, j: (i, j)),
        out_shape=jax.ShapeDtypeStruct((t, MLA_HEADS * MLA_QK_PAD), BF16),
        scratch_shapes=[pltpu.VMEM((bm, q_lora), BF16)],
        compiler_params=_params(2),
        name="mla_q_proj",
    )(p, g.reshape(1, q_lora), wq, cos, sin)


def _kvproj_kernel(c_ref, kr_ref, g_ref, wk_ref, wv_ref, cos_ref, sin_ref, k_ref, v_ref):
    xn = _rms_norm(c_ref[...], g_ref[...]).astype(BF16)
    kn = jnp.dot(xn, wk_ref[...], preferred_element_type=F32)
    vn = jnp.dot(xn, wv_ref[...], preferred_element_type=F32)
    k_rope = _rope_rotate(kr_ref[...], cos_ref[...], sin_ref[...]).astype(BF16)
    ones = jnp.ones((k_ref.shape[0], MLA_V_PAD - MLA_V), BF16)
    for h in range(MLA_HEADS):
        lo = h * MLA_QK_PAD
        k_ref[:, lo:lo + MLA_NOPE] = kn[:, h * MLA_NOPE:(h + 1) * MLA_NOPE].astype(BF16)
        k_ref[:, lo + MLA_NOPE:lo + MLA_QK_PAD] = k_rope
        lo = h * MLA_V_PAD
        v_ref[:, lo:lo + MLA_V] = vn[:, h * MLA_V:(h + 1) * MLA_V].astype(BF16)
        v_ref[:, lo + MLA_V:lo + MLA_V_PAD] = ones


def _kv_proj(p, c_off, kv_lora, kr_off, g, wk, wv, cos, sin, groups, bm):
    t = p.shape[0]
    pos = _pos_block_map(groups, bm)
    nk, nv = MLA_HEADS * MLA_QK_PAD, MLA_HEADS * MLA_V
    return pl.pallas_call(
        _kvproj_kernel,
        grid=(t // bm,),
        in_specs=[pl.BlockSpec((bm, kv_lora), lambda i: (i, c_off // kv_lora)),
                  pl.BlockSpec((bm, LANES), lambda i: (i, kr_off // LANES)),
                  pl.BlockSpec((1, kv_lora), lambda i: (0, 0)),
                  pl.BlockSpec((kv_lora, MLA_HEADS * MLA_NOPE), lambda i: (0, 0)),
                  pl.BlockSpec((kv_lora, nv), lambda i: (0, 0)),
                  pl.BlockSpec((bm, LANES), lambda i: (pos(i), 0)),
                  pl.BlockSpec((bm, LANES), lambda i: (pos(i), 0))],
        out_specs=[pl.BlockSpec((bm, nk), lambda i: (i, 0)),
                   pl.BlockSpec((bm, MLA_HEADS * MLA_V_PAD), lambda i: (i, 0))],
        out_shape=[jax.ShapeDtypeStruct((t, nk), BF16),
                   jax.ShapeDtypeStruct((t, MLA_HEADS * MLA_V_PAD), BF16)],
        compiler_params=_params(1),
        name="mla_kv_proj",
    )(p, p, g.reshape(1, kv_lora), wk, wv, cos, sin)


def _group_call(kernel, *, grid, in_specs, out_spec, out, args, more_out_specs=(),
                more_out_shapes=(), scratch_shapes=(), name):
    def body(*refs):
        n_in = len(in_specs)
        kernel(*refs[:n_in], *refs[n_in + 1:])

    return pl.pallas_call(
        body,
        grid=grid,
        in_specs=list(in_specs) + [pl.BlockSpec(memory_space=pl.ANY)],
        out_specs=[out_spec, *more_out_specs],
        out_shape=[jax.ShapeDtypeStruct(out.shape, out.dtype), *more_out_shapes],
        input_output_aliases={len(in_specs): 0},
        scratch_shapes=list(scratch_shapes),
        compiler_params=_params(len(grid)),
        name=name,
    )(*args, out)


def _softmax_pv(s, v):
    m = jnp.max(s, -1, keepdims=True)
    e = jnp.exp(s - m)
    l = jnp.sum(e, -1, keepdims=True)
    return jnp.dot(e.astype(BF16), v, preferred_element_type=F32) / l


def _mla_attn_chain(q_ref, k_ref, v_ref, o_ref, c):
    rows = pl.ds(c * MLA_Q_ROWS, MLA_Q_ROWS)
    s = lax.dot_general(q_ref[rows, :], k_ref[...], NT_DIMS, preferred_element_type=F32)
    e = jnp.exp2(s - jnp.max(s, -1, keepdims=True)).astype(BF16)
    acc = jnp.dot(e, v_ref[...], preferred_element_type=F32)
    o_ref[rows, :] = (acc[:, :MLA_V] / acc[:, MLA_V:]).astype(o_ref.dtype)


def _attn_gla_kernel(*refs, chains, n_gla_in, gla_kernel):
    q_ref, k_ref, v_ref = refs[:3]
    gla_in = refs[3:3 + n_gla_in]
    o_attn_ref, o_gla_ref, st_ref = refs[3 + n_gla_in:]
    gla_parts = gla_kernel(*gla_in, o_gla_ref, st_ref)
    next(gla_parts)
    every = max(1, chains // GLA_HEADS)
    for c in range(chains):
        if c % every == 0:
            next(gla_parts, None)
        _mla_attn_chain(q_ref, k_ref, v_ref, o_attn_ref, c)
    for _ in gla_parts:
        pass


def _mixer_attention(q, k, v, groups, gla_pass):
    t = q.shape[0]
    n_steps = t // GLA_SUPER
    out = jnp.zeros((t, MLA_HEADS * MLA_V), BF16)
    gla_out = None
    for g, (row_off, batch, seq) in enumerate(groups):
        bq = batch * seq * MLA_HEADS // n_steps
        assert seq % bq == 0 and bq % MLA_Q_ROWS == 0 and bq // MLA_Q_ROWS <= 8
        nq = seq // bq
        step = lambda b, h, i, nq=nq: (b * MLA_HEADS + h) * nq + i
        q_row = lambda b, h, i, nq=nq, o=row_off // bq: (o + b * nq + i, h)
        kv_row = lambda b, h, i, o=row_off // seq: (o + b, h)
        job = gla_pass(g, step, gla_out)
        out, gla_out = _group_call(
            functools.partial(_attn_gla_kernel, chains=bq // MLA_Q_ROWS,
                              n_gla_in=len(job["in_specs"]), gla_kernel=job["kernel"]),
            grid=(batch, MLA_HEADS, nq),
            in_specs=[pl.BlockSpec((bq, MLA_QK_PAD), q_row),
                      pl.BlockSpec((seq, MLA_QK_PAD), kv_row),
                      pl.BlockSpec((seq, MLA_V_PAD), kv_row)] + job["in_specs"],
            out_spec=pl.BlockSpec((bq, MLA_V), q_row),
            out=out, args=(q, k, v, *job["args"]),
            more_out_specs=[job["out_spec"]], more_out_shapes=[job["out_shape"]],
            scratch_shapes=job["scratch_shapes"], name="mla_attention_gla")
    return out, gla_out


GLA_SUPER = 4 * GLA_CHUNK


def _cumsum_rows(tri_b, x):
    hi = x.astype(BF16)
    r1 = x - hi.astype(F32)
    mid = r1.astype(BF16)
    lo = (r1 - mid.astype(F32)).astype(BF16)
    dot = lambda t: jnp.dot(tri_b, t, preferred_element_type=F32)
    return dot(hi) + dot(mid) + dot(lo)


def _gla_kernel(*refs, reverse, is_first, final, step):
    if final:
        (q_ref, k_ref, v_ref, lr_ref, w2_ref, gb_ref, fwd_ref, gr_ref, ng_ref,
         o_ref, st_ref) = refs
    else:
        q_ref, k_ref, v_ref, lr_ref, w2_ref, gb_ref, o_ref, st_ref = refs
    c, n_rows = GLA_CHUNK, GLA_SUPER
    nb = n_rows // c

    @pl.when(is_first(step(*(pl.program_id(a) for a in range(3)))))
    def _():
        st_ref[...] = jnp.zeros_like(st_ref)

    gate = jnp.dot(lr_ref[...].astype(BF16), w2_ref[...], preferred_element_type=F32) + gb_ref[...]
    log_a_all = ((jnp.minimum(gate, 0.0) - jnp.log1p(jnp.exp(-jnp.abs(gate))))
                 * (np.log2(np.e) / GLA_GATE_NORM))

    row = lax.broadcasted_iota(jnp.int32, (n_rows, n_rows), 0)
    col = lax.broadcasted_iota(jnp.int32, (n_rows, n_rows), 1)
    before = (col > row) if reverse else (col < row)
    upto = (col >= row) if reverse else (col <= row)
    same_chunk = (row // c) == (col // c)
    same_pair = (row // (2 * c)) == (col // (2 * c))
    m_chunk = jnp.logical_and(same_chunk, upto)
    m_pair = jnp.logical_and(jnp.logical_and(same_pair, jnp.logical_not(same_chunk)), before)
    m_cross = jnp.logical_and(jnp.logical_not(same_pair), before)

    tri_b = upto.astype(BF16)
    order = list(reversed(range(nb))) if reverse else list(range(nb))
    per_chunk = lambda r: jnp.concatenate(
        [jnp.broadcast_to(r[b], (c, GLA_DK)) for b in range(nb)], axis=0)
    nt = lambda a, b: lax.dot_general(a, b, NT_DIMS, preferred_element_type=F32)

    yield

    for h in range(GLA_HEADS):
        dk = slice(h * GLA_DK, (h + 1) * GLA_DK)
        dv = slice(h * GLA_DV, (h + 1) * GLA_DV)
        cum = _cumsum_rows(tri_b, log_a_all[:, dk])
        r_start, r_end = {}, {}
        r_prev = jnp.zeros((1, GLA_DK), F32)
        for b in order:
            e = b * c if reverse else b * c + c - 1
            r_start[b] = r_prev
            r_end[b] = cum[e:e + 1]
            r_prev = r_end[b]
        total = r_prev
        mid = r_end[order[nb // 2 - 1]]
        cumloc = cum - per_chunk(r_start)
        to_end = per_chunk(r_end) - cum

        q = q_ref[:, dk] * (GLA_DK ** -0.5)
        k = k_ref[:, dk]
        v = v_ref[:, dv].astype(BF16)
        q_loc = (q * jnp.exp2(cumloc)).astype(BF16)
        k_loc = (k * jnp.exp2(-cumloc)).astype(BF16)
        k_end = (k * jnp.exp2(to_end)).astype(BF16)
        q_mid = (q * jnp.exp2(jnp.minimum(cum - mid, 0.0))).astype(BF16)
        k_mid = (k * jnp.exp2(jnp.minimum(mid - cum, 0.0))).astype(BF16)
        q_all = (q * jnp.exp2(cum)).astype(BF16)
        k_all = (k * jnp.exp2(total - cum)).astype(BF16)

        attn = jnp.where(m_chunk, nt(q_loc, k_loc),
                         jnp.where(m_pair, nt(q_loc, k_end),
                                   jnp.where(m_cross, nt(q_mid, k_mid), 0.0))).astype(BF16)
        state = st_ref[h]
        o = jnp.dot(attn, v, preferred_element_type=F32) + nt(q_all, state.astype(BF16))
        st_ref[h] = state * jnp.exp2(total) + lax.dot_general(
            v, k_all, TN_DIMS, preferred_element_type=F32)
        if final:
            tot = fwd_ref[:, dv] + o
            gr = gr_ref[:, dv]
            o_ref[:, dv] = (_rms_norm(tot, ng_ref[...]) * (gr * _sigmoid(gr))).astype(o_ref.dtype)
        else:
            o_ref[:, dv] = o
        yield


def _gla_job(p, p_glr, offs, w2cat, gbcat, groups, step, *, reverse, fwd=None, norm_g=None):
    t = p.shape[0]
    final = fwd is not None
    lb = GLA_SUPER
    (_, _, seq_a), (off_b, _, seq_b) = groups
    n_a, ns_a, ns_b = off_b // lb, seq_a // lb, seq_b // lb
    assert seq_a % lb == 0 and seq_b % lb == 0
    d = 1 if reverse else 0

    def local(u):
        in_a = u < n_a
        return jnp.where(in_a, u % ns_a, (u - n_a) % ns_b), jnp.where(in_a, ns_a, ns_b)

    def rows(u):
        n, ns = local(u)
        return (u - n) + (ns - 1 - n) if reverse else u

    is_first = lambda u: local(u)[0] == 0
    key, val = GLA_HEADS * GLA_DK, GLA_HEADS * GLA_DV
    assert offs["gq"] % key == 0 and offs["gk"] % key == 0
    assert offs["gv"] % val == 0 and offs["gr"] % val == 0
    col = lambda off, width: (lambda *ids: (rows(step(*ids)), off // width))
    in_specs = [pl.BlockSpec((lb, key), col(offs["gq"], key)),
                pl.BlockSpec((lb, key), col(offs["gk"], key)),
                pl.BlockSpec((lb, val), col(offs["gv"], val)),
                pl.BlockSpec((lb, LANES), col(0, LANES)),
                pl.BlockSpec((LANES, key), lambda *ids: (0, d)),
                pl.BlockSpec((1, key), lambda *ids: (0, d))]
    args = [p, p, p, p_glr, w2cat, gbcat]
    if final:
        in_specs += [pl.BlockSpec((lb, val), col(0, val)),
                     pl.BlockSpec((lb, val), col(offs["gr"], val)),
                     pl.BlockSpec((1, GLA_DV), lambda *ids: (0, 0))]
        args += [fwd, p, norm_g.reshape(1, GLA_DV)]
    return dict(
        kernel=functools.partial(_gla_kernel, reverse=reverse, is_first=is_first, final=final,
                                 step=step),
        in_specs=in_specs, args=args,
        out_spec=pl.BlockSpec((lb, val), col(0, val)),
        out_shape=jax.ShapeDtypeStruct((t, val), BF16 if final else F32),
        scratch_shapes=[pltpu.VMEM((GLA_HEADS, GLA_DV, GLA_DK), F32)])


def _merge_kernel(a1_ref, a2_ref, w1_ref, w2_ref, g1_ref, g2_ref, b1_ref, b2_ref, o_ref):
    y1 = jnp.dot(a1_ref[...], w1_ref[...], preferred_element_type=F32)
    y2 = jnp.dot(a2_ref[...], w2_ref[...], preferred_element_type=F32)
    s1 = _sigmoid(g1_ref[...] + b1_ref[...])
    s2 = _sigmoid(g2_ref[...] + b2_ref[...])
    o_ref[...] = (s1 * y1 + s2 * y2).astype(o_ref.dtype)


def _branch_merge(o_mla, o_gla, w_mla, w_gla, p, gm_off, b_merge):
    t, k1 = o_mla.shape
    k2 = o_gla.shape[1]
    d = w_mla.shape[1]
    bm, bn = _block(t, 1024), _block(np.gcd(d, gm_off), 1024)
    g0, nd = gm_off // bn, d // bn
    return pl.pallas_call(
        _merge_kernel,
        grid=(t // bm, nd),
        in_specs=[pl.BlockSpec((bm, k1), lambda i, j: (i, 0)),
                  pl.BlockSpec((bm, k2), lambda i, j: (i, 0)),
                  pl.BlockSpec((k1, bn), lambda i, j: (0, j)),
                  pl.BlockSpec((k2, bn), lambda i, j: (0, j)),
                  pl.BlockSpec((bm, bn), lambda i, j: (i, g0 + j)),
                  pl.BlockSpec((bm, bn), lambda i, j: (i, g0 + nd + j)),
                  pl.BlockSpec((1, bn), lambda i, j: (0, j)),
                  pl.BlockSpec((1, bn), lambda i, j: (0, nd + j))],
        out_specs=pl.BlockSpec((bm, bn), lambda i, j: (i, j)),
        out_shape=jax.ShapeDtypeStruct((t, d), BF16),
        compiler_params=_params(2, VMEM_LIMIT_LARGE_BYTES),
        name="branch_merge",
    )(o_mla, o_gla, w_mla, w_gla, p, p, b_merge.reshape(1, 2 * d), b_merge.reshape(1, 2 * d))


def _xattn_kernel(q_ref, k_ref, v_ref, o_ref, *, scale):
    hd = q_ref.shape[1] // XA_HEADS
    for h in range(XA_HEADS):
        cols = slice(h * hd, (h + 1) * hd)
        s = lax.dot_general(q_ref[:, cols], k_ref[:, cols], NT_DIMS,
                            preferred_element_type=F32) * scale
        o_ref[:, cols] = _softmax_pv(s, v_ref[:, cols]).astype(o_ref.dtype)


def _cross_attention(q, kv, groups, n_mem):
    t, d = q.shape
    (_, batch_a, seq_a), (off_b, _, seq_b) = groups
    bq = _block(np.gcd(seq_a, seq_b), 512)
    n_a, per_a, per_b = off_b // bq, seq_a // bq, seq_b // bq
    mem_row = lambda u: jnp.where(u < n_a, u // per_a, batch_a + (u - n_a) // per_b)
    return pl.pallas_call(
        functools.partial(_xattn_kernel, scale=(d // XA_HEADS) ** -0.5),
        grid=(t // bq,),
        in_specs=[pl.BlockSpec((bq, d), lambda u: (u, 0)),
                  pl.BlockSpec((n_mem, d), lambda u: (mem_row(u), 0)),
                  pl.BlockSpec((n_mem, d), lambda u: (mem_row(u), 1))],
        out_specs=pl.BlockSpec((bq, d), lambda u: (u, 0)),
        out_shape=jax.ShapeDtypeStruct((t, d), BF16),
        compiler_params=_params(1),
        name="cross_attention",
    )(q, kv, kv)


def _in_proj_layout(d_model, q_lora, kv_lora):
    gla_key, gla_val = GLA_HEADS * GLA_DK, GLA_HEADS * GLA_DV
    b_width = 2 * gla_key + 2 * gla_val
    s_kr = q_lora + kv_lora
    s_gq = s_kr + MLA_ROPE
    s_glr = s_gq + b_width
    s_gm = s_glr + 2 * GLA_GATE_RANK
    pb = 1024 if b_width % 1024 == 0 and (2 * d_model) % 1024 == 0 else 512
    assert b_width % pb == 0 and (2 * d_model) % pb == 0
    assert s_kr % LANES == 0 and q_lora % kv_lora == 0 and 2 * GLA_GATE_RANK <= LANES
    assert s_gq % 8 == 0 and s_glr % 8 == 0 and s_gm % 8 == 0
    n_a = -(-s_gq // pb)
    gq = n_a * pb
    starts = ([pb * j for j in range(n_a)] + [s_gq + pb * j for j in range(b_width // pb)]
              + [s_gm + pb * j for j in range(2 * d_model // pb)])
    offs = {"cq": 0, "ckv": q_lora, "krope": s_kr, "gq": gq, "gk": gq + gla_key,
            "gv": gq + 2 * gla_key, "gr": gq + 2 * gla_key + gla_val, "gm": gq + b_width}
    return offs, starts, pb, s_glr


BF16_SUBLANES = 16


def _in_proj_kernel(*refs, n_cast, cast_steps, n_cols):
    a_ref, wt_ref, wg_ref = refs[:3]
    cast_in = refs[3:3 + n_cast]
    o_ref, og_ref = refs[3 + n_cast:5 + n_cast]
    cast_out = refs[5 + n_cast:]
    o_ref[...] = lax.dot_general(a_ref[...], wt_ref[...], NT_DIMS, preferred_element_type=F32)

    @pl.when(pl.program_id(1) == 0)
    def _():
        og_ref[...] = lax.dot_general(a_ref[...], wg_ref[...], NT_DIMS,
                                      preferred_element_type=F32)

    step = pl.program_id(0) * n_cols + pl.program_id(1)
    for src, dst, active in zip(cast_in, cast_out, cast_steps):
        @pl.when(step < active)
        def _(src=src, dst=dst):
            dst[...] = src[...].astype(BF16)


def _in_proj(a, w_t, starts, bn, gate_start, *, bm, name, cast_weights):
    m, kdim = a.shape
    bm = _block(m, bm)
    n_cols = len(starts)
    n_steps = (m // bm) * n_cols

    assert all(s % BF16_SUBLANES == 0 for s in starts) and gate_start % BF16_SUBLANES == 0

    def row_start(j):
        r = jnp.int32(starts[-1] // BF16_SUBLANES)
        for idx in range(n_cols - 2, -1, -1):
            r = jnp.where(j == idx, jnp.int32(starts[idx] // BF16_SUBLANES), r)
        return pl.multiple_of(r * BF16_SUBLANES, BF16_SUBLANES)

    cast_specs, cast_steps = [], []
    for w in cast_weights:
        rows = BF16_SUBLANES
        while w.shape[0] % rows or w.shape[0] // rows > n_steps:
            rows += BF16_SUBLANES
        active = w.shape[0] // rows
        cast_steps.append(active)
        cast_specs.append(pl.BlockSpec(
            (rows, w.shape[1]),
            lambda i, j, active=active: (jnp.minimum(i * n_cols + j, active - 1), 0)))

    out = pl.pallas_call(
        functools.partial(_in_proj_kernel, n_cast=len(cast_weights),
                          cast_steps=tuple(cast_steps), n_cols=n_cols),
        grid=(m // bm, n_cols),
        in_specs=[pl.BlockSpec((bm, kdim), lambda i, j: (i, 0)),
                  pl.BlockSpec((pl.Element(bn), pl.Element(kdim)),
                               lambda i, j: (row_start(j), 0)),
                  pl.BlockSpec((pl.Element(LANES), pl.Element(kdim)),
                               lambda i, j: (gate_start, 0))] + cast_specs,
        out_specs=[pl.BlockSpec((bm, bn), lambda i, j: (i, j)),
                   pl.BlockSpec((bm, LANES), lambda i, j: (i, 0))] + cast_specs,
        out_shape=[jax.ShapeDtypeStruct((m, bn * n_cols), F32),
                   jax.ShapeDtypeStruct((m, LANES), F32)]
        + [jax.ShapeDtypeStruct(w.shape, BF16) for w in cast_weights],
        compiler_params=_params(2, VMEM_LIMIT_LARGE_BYTES),
        name=name,
    )(a, w_t, w_t, *cast_weights)
    return out[0], out[1], tuple(out[2:])


def _pack_uq(w_uq):
    q_lora = w_uq.shape[0]
    w = w_uq.reshape(q_lora, MLA_HEADS, MLA_NOPE + MLA_ROPE)
    rope = w[:, :, MLA_NOPE:]
    half = MLA_ROPE // 2
    rot = jnp.concatenate([-rope[:, :, half:], rope[:, :, :half]], axis=2)
    return jnp.concatenate([w, rot], axis=2).reshape(q_lora, MLA_HEADS * MLA_QK_PAD).astype(BF16)


def _pack_ukv(w_ukv):
    kv_lora = w_ukv.shape[0]
    w = w_ukv.reshape(kv_lora, MLA_HEADS, MLA_NOPE + MLA_V)
    wk = w[:, :, :MLA_NOPE].reshape(kv_lora, MLA_HEADS * MLA_NOPE)
    wv = w[:, :, MLA_NOPE:].reshape(kv_lora, MLA_HEADS * MLA_V)
    return wk.astype(BF16), wv.astype(BF16)


def _pack_gate(w2, gb):
    r, key = w2.shape[1], w2.shape[2]
    z = jnp.zeros((r, key), w2.dtype)
    top = jnp.concatenate([w2[0], z], axis=1)
    bot = jnp.concatenate([z, w2[1]], axis=1)
    tail = jnp.zeros((LANES - 2 * r, 2 * key), w2.dtype)
    return jnp.concatenate([top, bot, tail], axis=0).astype(BF16), gb.reshape(1, 2 * key)


def kernel(x_prompt, x_sample, mem_prompt, mem_sample, ln_in_g, ln_in_b, w_in, b_merge, mla_q_norm, w_uq, mla_kv_norm, w_ukv, gla_gate_w2, gla_gate_b, gla_norm, w_branch_mla, w_branch_gla, w_mix_out, ln1_g, ln1_b, xa_wq, xa_wkv, xa_wo, ln2_g, ln2_b, mlp_w1, mlp_w2, ln3_g, ln3_b):
    assert w_in.shape[0] == DEPTH
    ba, sa, d = x_prompt.shape
    bb, sb, _ = x_sample.shape
    n_mem = mem_prompt.shape[1]
    ta, tb = ba * sa, bb * sb
    groups = ((0, ba, sa), (ta, bb, sb))
    q_lora, kv_lora = mla_q_norm.shape[1], mla_kv_norm.shape[1]
    assert ta % sb == 0 and ta % 1024 == 0 and sa % 1024 == 0 and sb % 1024 == 0

    offs, in_starts, in_bn, glr_start = _in_proj_layout(d, q_lora, kv_lora)
    w_in_t = jnp.swapaxes(w_in[0], 0, 1).astype(BF16)
    wq_p = _pack_uq(w_uq[0])
    wk_p, wv_p = _pack_ukv(w_ukv[0])
    w2cat, gbcat = _pack_gate(gla_gate_w2[0], gla_gate_b[0])
    cos, sin = _rope_tables(max(sa, sb))

    h, h_b = _ln_in(x_prompt.reshape(ta, d), x_sample.reshape(tb, d), ln_in_g, ln_in_b)

    later_weights = (w_branch_mla[0], w_branch_gla[0], w_mix_out[0], xa_wq[0], xa_wkv[0],
                     xa_wo[0], mlp_w1[0], mlp_w2[0])
    (p, p_glr,
     (w_br_mla_b, w_br_gla_b, w_mix_b, xa_wq_b, xa_wkv_b, xa_wo_b, mlp_w1_b, mlp_w2_b)) = _in_proj(
        h_b, w_in_t, in_starts, in_bn, glr_start, bm=1024, name="in_proj",
        cast_weights=later_weights)
    q = _q_proj(p, offs["cq"], q_lora, mla_q_norm[0], wq_p, cos, sin, groups, 1024)
    k, v = _kv_proj(p, offs["ckv"], kv_lora, offs["krope"], mla_kv_norm[0], wk_p, wv_p,
                    cos, sin, groups, 512)
    def gla_pass(g, step, fwd):
        return _gla_job(p, p_glr, offs, w2cat, gbcat, groups, step, reverse=g == 1,
                        fwd=fwd, norm_g=gla_norm[0])

    o_mla, o_gla = _mixer_attention(q, k, v, groups, gla_pass)
    merged = _branch_merge(o_mla, o_gla, w_br_mla_b, w_br_gla_b, p, offs["gm"], b_merge[0])
    h, h_b = _matmul_ln(merged, w_mix_b, h, ln1_g[0], ln1_b[0], bm=256, name="mix_out_ln")

    mem = jnp.concatenate([mem_prompt.reshape(ba * n_mem, d), mem_sample.reshape(bb * n_mem, d)])
    xq = _matmul(h_b, xa_wq_b, BF16, name="xa_q")
    xkv = _matmul(mem.astype(BF16), xa_wkv_b, BF16, bm=2048, name="xa_kv")
    o_x = _cross_attention(xq, xkv, groups, n_mem)
    h, h_b = _matmul_ln(o_x, xa_wo_b, h, ln2_g[0], ln2_b[0], bm=256, name="xa_o_ln")

    u = _matmul(h_b, mlp_w1_b, BF16, relu2=True, name="mlp_up")
    z = _matmul_res(u, mlp_w2_b, h, name="mlp_down")
    y_a = _ln_rows(z, ln3_g[0], ln3_b[0], row_off=0, rows=ta)
    y_b = _ln_rows(z, ln3_g[0], ln3_b[0], row_off=ta, rows=tb)
    return y_a.reshape(ba, sa, d), y_b.reshape(bb, sb, d)
```

```python
import functools

import numpy as np
import jax
import jax.numpy as jnp
from jax import lax
from jax.experimental import pallas as pl
from jax.experimental.pallas import tpu as pltpu

MLA_HEADS = 16
MLA_NOPE = 128
MLA_ROPE = 64
MLA_V = 128
ROPE_THETA = 10000.0
GLA_HEADS = 4
GLA_DK = 256
GLA_DV = 512
GLA_GATE_RANK = 16
GLA_GATE_NORM = 16.0
GLA_CHUNK = 64
XA_HEADS = 4
LN_EPS = 1e-5
RMS_EPS = 1e-6
DEPTH = 1
DN_ALPHA = (2.0 * DEPTH) ** 0.25

LANES = 128
MLA_QK_PAD = 2 * LANES
MLA_V_PAD = 2 * LANES
MLA_Q_ROWS = 256
VMEM_LIMIT_BYTES = 56 * 2**20
VMEM_LIMIT_LARGE_BYTES = 63 * 2**20

BF16 = jnp.bfloat16
F32 = jnp.float32
NT_DIMS = (((1,), (1,)), ((), ()))
TN_DIMS = (((0,), (0,)), ((), ()))


def _params(n_grid, vmem_limit_bytes=VMEM_LIMIT_BYTES):
    return pltpu.CompilerParams(dimension_semantics=("arbitrary",) * n_grid,
                                vmem_limit_bytes=vmem_limit_bytes)


def _block(n, pref):
    b = min(n, pref)
    while n % b:
        b //= 2
    return b


def _layer_norm(x, g, b):
    mu = jnp.mean(x, -1, keepdims=True)
    xc = x - mu
    var = jnp.mean(xc * xc, -1, keepdims=True)
    return xc * lax.rsqrt(var + LN_EPS) * g + b


def _rms_norm(x, g):
    return x * lax.rsqrt(jnp.mean(x * x, -1, keepdims=True) + RMS_EPS) * g


def _sigmoid(x):
    return 1.0 / (1.0 + jnp.exp(-x))


def _rope(x, cos, sin):
    return x * cos + pltpu.roll(x, MLA_ROPE, 1) * sin


def _rope_rotate(x, cos, sin):
    half = MLA_ROPE // 2
    lane = lax.broadcasted_iota(jnp.int32, x.shape, 1)
    rot = jnp.where(lane < half, -pltpu.roll(x, LANES - half, 1), pltpu.roll(x, half, 1))
    return x * cos + rot * sin


def _ln_in_kernel(xa_ref, xb_ref, g_ref, b_ref, h_ref, hb_ref, *, n_a):
    def emit(x_ref):
        y = _layer_norm(x_ref[...], g_ref[...], b_ref[...])
        h_ref[...] = y
        hb_ref[...] = y.astype(BF16)

    @pl.when(pl.program_id(0) < n_a)
    def _():
        emit(xa_ref)

    @pl.when(pl.program_id(0) >= n_a)
    def _():
        emit(xb_ref)


def _ln_in(xa, xb, g, b):
    ta, d = xa.shape
    tb = xb.shape[0]
    bm = _block(np.gcd(ta, tb), 256)
    n_a, n_b = ta // bm, tb // bm
    row = pl.BlockSpec((1, d), lambda i: (0, 0))
    out = pl.BlockSpec((bm, d), lambda i: (i, 0))
    return pl.pallas_call(
        functools.partial(_ln_in_kernel, n_a=n_a),
        grid=(n_a + n_b,),
        in_specs=[pl.BlockSpec((bm, d), lambda i: (jnp.minimum(i, n_a - 1), 0)),
                  pl.BlockSpec((bm, d), lambda i: (jnp.maximum(i - n_a, 0), 0)),
                  row, row],
        out_specs=[out, out],
        out_shape=[jax.ShapeDtypeStruct((ta + tb, d), F32),
                   jax.ShapeDtypeStruct((ta + tb, d), BF16)],
        compiler_params=_params(1),
        name="ln_in",
    )(xa, xb, g.reshape(1, d), b.reshape(1, d))


def _ln_rows_kernel(z_ref, g_ref, b_ref, o_ref):
    o_ref[...] = _layer_norm(z_ref[...], g_ref[...], b_ref[...])


def _ln_rows(z, g, b, *, row_off, rows):
    d = z.shape[1]
    bm = _block(np.gcd(rows, row_off) if row_off else rows, 256)
    off = row_off // bm
    row = pl.BlockSpec((1, d), lambda i: (0, 0))
    return pl.pallas_call(
        _ln_rows_kernel,
        grid=(rows // bm,),
        in_specs=[pl.BlockSpec((bm, d), lambda i: (i + off, 0)), row, row],
        out_specs=pl.BlockSpec((bm, d), lambda i: (i, 0)),
        out_shape=jax.ShapeDtypeStruct((rows, d), F32),
        compiler_params=_params(1),
        name="ln_out",
    )(z, g.reshape(1, d), b.reshape(1, d))


def _matmul_res_kernel(a_ref, b_ref, h_ref, o_ref, *scratch, nk):
    part = jnp.dot(a_ref[...], b_ref[...], preferred_element_type=F32)

    def finish(acc):
        o_ref[...] = DN_ALPHA * h_ref[...] + acc

    if nk == 1:
        finish(part)
        return
    acc_ref, = scratch
    k = pl.program_id(2)

    @pl.when(k == 0)
    def _():
        acc_ref[...] = part

    @pl.when(jnp.logical_and(k > 0, k < nk - 1))
    def _():
        acc_ref[...] += part

    @pl.when(k == nk - 1)
    def _():
        finish(acc_ref[...] + part)


def _matmul_res(a, b, h, *, bm=1024, bn=1024, bk=4096, name="matmul_res"):
    m, kdim = a.shape
    n = b.shape[1]
    bm, bn, bk = _block(m, bm), _block(n, bn), _block(kdim, bk)
    nk = kdim // bk
    tile = pl.BlockSpec((bm, bn), lambda i, j, k: (i, j))
    return pl.pallas_call(
        functools.partial(_matmul_res_kernel, nk=nk),
        grid=(m // bm, n // bn, nk),
        in_specs=[pl.BlockSpec((bm, bk), lambda i, j, k: (i, k)),
                  pl.BlockSpec((bk, bn), lambda i, j, k: (k, j)),
                  tile],
        out_specs=tile,
        out_shape=jax.ShapeDtypeStruct((m, n), F32),
        scratch_shapes=[pltpu.VMEM((bm, bn), F32)] if nk > 1 else [],
        compiler_params=_params(3, VMEM_LIMIT_LARGE_BYTES),
        name=name,
    )(a, b, h)


def _matmul_kernel(a_ref, b_ref, o_ref, *, relu2):
    acc = jnp.dot(a_ref[...], b_ref[...], preferred_element_type=F32)
    if relu2:
        acc = jnp.square(jnp.maximum(acc, 0.0))
    o_ref[...] = acc.astype(o_ref.dtype)


def _matmul_piped(a, b, out_dtype, *, bm=1024, bn=1024, relu2=False, name="matmul_piped"):
    m, kdim = a.shape
    n = b.shape[1]
    bm, bn = _block(m, bm), _block(n, bn)

    def outer(a_hbm, b_hbm, o_hbm):
        pltpu.emit_pipeline(
            functools.partial(_matmul_kernel, relu2=relu2),
            grid=(m // bm, n // bn),
            in_specs=[pl.BlockSpec((bm, kdim), lambda i, j: (i, 0)),
                      pl.BlockSpec((kdim, bn), lambda i, j: (0, j),
                                   pipeline_mode=pl.Buffered(3))],
            out_specs=[pl.BlockSpec((bm, bn), lambda i, j: (i, j))],
        )(a_hbm, b_hbm, o_hbm)

    whole = pl.BlockSpec(memory_space=pl.ANY)
    return pl.pallas_call(
        outer,
        in_specs=[whole, whole],
        out_specs=whole,
        out_shape=jax.ShapeDtypeStruct((m, n), out_dtype),
        compiler_params=pltpu.CompilerParams(vmem_limit_bytes=VMEM_LIMIT_BYTES),
        name=name,
    )(a, b)


def _matmul(a, b, out_dtype, *, bm=1024, bn=1024, relu2=False, name="matmul"):
    m, kdim = a.shape
    n = b.shape[1]
    bm, bn = _block(m, bm), _block(n, bn)
    return pl.pallas_call(
        functools.partial(_matmul_kernel, relu2=relu2),
        grid=(m // bm, n // bn),
        in_specs=[pl.BlockSpec((bm, kdim), lambda i, j: (i, 0)),
                  pl.BlockSpec((kdim, bn), lambda i, j: (0, j))],
        out_specs=pl.BlockSpec((bm, bn), lambda i, j: (i, j)),
        out_shape=jax.ShapeDtypeStruct((m, n), out_dtype),
        compiler_params=_params(2),
        name=name,
    )(a, b)


def _matmul_ln_kernel(a_ref, w_ref, h_ref, g_ref, b_ref, o_ref, ob_ref, *, chains):
    rows_per = a_ref.shape[0] // chains
    for c in range(chains):
        rows = pl.ds(c * rows_per, rows_per)
        y = jnp.dot(a_ref[rows, :], w_ref[...], preferred_element_type=F32)
        z = _layer_norm(DN_ALPHA * h_ref[rows, :] + y, g_ref[...], b_ref[...])
        o_ref[rows, :] = z
        ob_ref[rows, :] = z.astype(BF16)


def _matmul_ln(a, w, h, g, b, *, bm, name):
    m, kdim = a.shape
    n = w.shape[1]
    bm = _block(m, bm)
    row = pl.BlockSpec((1, n), lambda i: (0, 0))
    tile = pl.BlockSpec((bm, n), lambda i: (i, 0))
    return pl.pallas_call(
        functools.partial(_matmul_ln_kernel, chains=2 if bm % 256 == 0 else 1),
        grid=(m // bm,),
        in_specs=[pl.BlockSpec((bm, kdim), lambda i: (i, 0)),
                  pl.BlockSpec((kdim, n), lambda i: (0, 0), pipeline_mode=pl.Buffered(1)),
                  tile, row, row],
        out_specs=[tile, tile],
        out_shape=[jax.ShapeDtypeStruct((m, n), F32), jax.ShapeDtypeStruct((m, n), BF16)],
        compiler_params=_params(1, VMEM_LIMIT_LARGE_BYTES),
        name=name,
    )(a, w, h, g.reshape(1, n), b.reshape(1, n))


def _rope_tables(seq_len):
    half = MLA_ROPE // 2
    inv = 1.0 / (ROPE_THETA ** (jnp.arange(0, MLA_ROPE, 2, dtype=F32) / MLA_ROPE))
    ang = jnp.arange(seq_len, dtype=F32)[:, None] * inv[None, :]
    zero = jnp.zeros((seq_len, LANES - 2 * half), F32)
    cos = jnp.concatenate([jnp.cos(ang), jnp.cos(ang), zero], axis=1)
    sin = jnp.concatenate([jnp.sin(ang), jnp.sin(ang), zero], axis=1)
    return cos, sin


def _pos_block_map(groups, bm):
    (_, _, s_a), (off_b, _, s_b) = groups
    n_a = off_b // bm

    def index(i):
        return jnp.where(i < n_a, i % (s_a // bm), (i - n_a) % (s_b // bm))
    return index


def _qproj_kernel(c_ref, g_ref, w_ref, cos_ref, sin_ref, o_ref, xn_ref, *, heads, scale):
    @pl.when(pl.program_id(1) == 0)
    def _():
        xn_ref[...] = _rms_norm(c_ref[...], g_ref[...]).astype(BF16)

    r = jnp.dot(xn_ref[...], w_ref[...], preferred_element_type=F32)
    cos, sin = cos_ref[...], sin_ref[...]
    for h in range(heads):
        lo = h * MLA_QK_PAD
        o_ref[:, lo:lo + MLA_NOPE] = (r[:, lo:lo + MLA_NOPE] * scale).astype(BF16)
        o_ref[:, lo + MLA_NOPE:lo + MLA_QK_PAD] = (
            _rope(r[:, lo + MLA_NOPE:lo + MLA_QK_PAD], cos, sin) * scale).astype(BF16)


def _q_proj(p, c_off, q_lora, g, wq, cos, sin, groups, bm):
    t = p.shape[0]
    heads = min(8, MLA_HEADS)
    bn = heads * MLA_QK_PAD
    pos = _pos_block_map(groups, bm)
    scale = (MLA_NOPE + MLA_ROPE) ** -0.5 * np.log2(np.e)
    return pl.pallas_call(
        functools.partial(_qproj_kernel, heads=heads, scale=scale),
        grid=(t // bm, MLA_HEADS // heads),
        in_specs=[pl.BlockSpec((bm, q_lora), lambda i, j: (i, c_off // q_lora)),
                  pl.BlockSpec((1, q_lora), lambda i, j: (0, 0)),
                  pl.BlockSpec((q_lora, bn), lambda i, j: (0, j)),
                  pl.BlockSpec((bm, LANES), lambda i, j: (pos(i), 0)),
                  pl.BlockSpec((bm, LANES), lambda i, j: (pos(i), 0))],
        out_specs=pl.BlockSpec((bm, bn), lambda i, j: (i, j)),
        out_shape=jax.ShapeDtypeStruct((t, MLA_HEADS * MLA_QK_PAD), BF16),
        scratch_shapes=[pltpu.VMEM((bm, q_lora), BF16)],
        compiler_params=_params(2),
        name="mla_q_proj",
    )(p, g.reshape(1, q_lora), wq, cos, sin)


def _kvproj_kernel(c_ref, kr_ref, g_ref, wk_ref, wv_ref, cos_ref, sin_ref, k_ref, v_ref):
    xn = _rms_norm(c_ref[...], g_ref[...]).astype(BF16)
    kn = jnp.dot(xn, wk_ref[...], preferred_element_type=F32)
    vn = jnp.dot(xn, wv_ref[...], preferred_element_type=F32)
    k_rope = _rope_rotate(kr_ref[...], cos_ref[...], sin_ref[...]).astype(BF16)
    ones = jnp.ones((k_ref.shape[0], MLA_V_PAD - MLA_V), BF16)
    for h in range(MLA_HEADS):
        lo = h * MLA_QK_PAD
        k_ref[:, lo:lo + MLA_NOPE] = kn[:, h * MLA_NOPE:(h + 1) * MLA_NOPE].astype(BF16)
        k_ref[:, lo + MLA_NOPE:lo + MLA_QK_PAD] = k_rope
        lo = h * MLA_V_PAD
        v_ref[:, lo:lo + MLA_V] = vn[:, h * MLA_V:(h + 1) * MLA_V].astype(BF16)
        v_ref[:, lo + MLA_V:lo + MLA_V_PAD] = ones


def _kv_proj(p, c_off, kv_lora, kr_off, g, wk, wv, cos, sin, groups, bm):
    t = p.shape[0]
    pos = _pos_block_map(groups, bm)
    nk, nv = MLA_HEADS * MLA_QK_PAD, MLA_HEADS * MLA_V
    return pl.pallas_call(
        _kvproj_kernel,
        grid=(t // bm,),
        in_specs=[pl.BlockSpec((bm, kv_lora), lambda i: (i, c_off // kv_lora)),
                  pl.BlockSpec((bm, LANES), lambda i: (i, kr_off // LANES)),
                  pl.BlockSpec((1, kv_lora), lambda i: (0, 0)),
                  pl.BlockSpec((kv_lora, MLA_HEADS * MLA_NOPE), lambda i: (0, 0)),
                  pl.BlockSpec((kv_lora, nv), lambda i: (0, 0)),
                  pl.BlockSpec((bm, LANES), lambda i: (pos(i), 0)),
                  pl.BlockSpec((bm, LANES), lambda i: (pos(i), 0))],
        out_specs=[pl.BlockSpec((bm, nk), lambda i: (i, 0)),
                   pl.BlockSpec((bm, MLA_HEADS * MLA_V_PAD), lambda i: (i, 0))],
        out_shape=[jax.ShapeDtypeStruct((t, nk), BF16),
                   jax.ShapeDtypeStruct((t, MLA_HEADS * MLA_V_PAD), BF16)],
        compiler_params=_params(1),
        name="mla_kv_proj",
    )(p, p, g.reshape(1, kv_lora), wk, wv, cos, sin)


def _group_call(kernel, *, grid, in_specs, out_spec, out, args, more_out_specs=(),
                more_out_shapes=(), scratch_shapes=(), name):
    def body(*refs):
        n_in = len(in_specs)
        kernel(*refs[:n_in], *refs[n_in + 1:])

    return pl.pallas_call(
        body,
        grid=grid,
        in_specs=list(in_specs) + [pl.BlockSpec(memory_space=pl.ANY)],
        out_specs=[out_spec, *more_out_specs],
        out_shape=[jax.ShapeDtypeStruct(out.shape, out.dtype), *more_out_shapes],
        input_output_aliases={len(in_specs): 0},
        scratch_shapes=list(scratch_shapes),
        compiler_params=_params(len(grid)),
        name=name,
    )(*args, out)


def _softmax_pv(s, v):
    m = jnp.max(s, -1, keepdims=True)
    e = jnp.exp(s - m)
    l = jnp.sum(e, -1, keepdims=True)
    return jnp.dot(e.astype(BF16), v, preferred_element_type=F32) / l


def _mla_attn_chain(q_ref, k_ref, v_ref, o_ref, c):
    rows = pl.ds(c * MLA_Q_ROWS, MLA_Q_ROWS)
    s = lax.dot_general(q_ref[rows, :], k_ref[...], NT_DIMS, preferred_element_type=F32)
    e = jnp.exp2(s - jnp.max(s, -1, keepdims=True)).astype(BF16)
    acc = jnp.dot(e, v_ref[...], preferred_element_type=F32)
    o_ref[rows, :] = (acc[:, :MLA_V] / acc[:, MLA_V:]).astype(o_ref.dtype)


def _attn_gla_kernel(*refs, chains, n_gla_in, gla_kernel):
    q_ref, k_ref, v_ref = refs[:3]
    gla_in = refs[3:3 + n_gla_in]
    o_attn_ref, o_gla_ref, st_ref = refs[3 + n_gla_in:]
    gla_parts = gla_kernel(*gla_in, o_gla_ref, st_ref)
    next(gla_parts)
    every = max(1, chains // GLA_HEADS)
    for c in range(chains):
        if c % every == 0:
            next(gla_parts, None)
        _mla_attn_chain(q_ref, k_ref, v_ref, o_attn_ref, c)
    for _ in gla_parts:
        pass


def _mixer_attention(q, k, v, groups, gla_pass):
    t = q.shape[0]
    n_steps = t // GLA_SUPER
    out = jnp.zeros((t, MLA_HEADS * MLA_V), BF16)
    gla_out = None
    for g, (row_off, batch, seq) in enumerate(groups):
        bq = batch * seq * MLA_HEADS // n_steps
        assert seq % bq == 0 and bq % MLA_Q_ROWS == 0 and bq // MLA_Q_ROWS <= 8
        nq = seq // bq
        step = lambda b, h, i, nq=nq: (b * MLA_HEADS + h) * nq + i
        q_row = lambda b, h, i, nq=nq, o=row_off // bq: (o + b * nq + i, h)
        kv_row = lambda b, h, i, o=row_off // seq: (o + b, h)
        job = gla_pass(g, step, gla_out)
        out, gla_out = _group_call(
            functools.partial(_attn_gla_kernel, chains=bq // MLA_Q_ROWS,
                              n_gla_in=len(job["in_specs"]), gla_kernel=job["kernel"]),
            grid=(batch, MLA_HEADS, nq),
            in_specs=[pl.BlockSpec((bq, MLA_QK_PAD), q_row),
                      pl.BlockSpec((seq, MLA_QK_PAD), kv_row),
                      pl.BlockSpec((seq, MLA_V_PAD), kv_row)] + job["in_specs"],
            out_spec=pl.BlockSpec((bq, MLA_V), q_row),
            out=out, args=(q, k, v, *job["args"]),
            more_out_specs=[job["out_spec"]], more_out_shapes=[job["out_shape"]],
            scratch_shapes=job["scratch_shapes"], name="mla_attention_gla")
    return out, gla_out


GLA_SUPER = 4 * GLA_CHUNK


def _cumsum_rows(tri_b, x):
    hi = x.astype(BF16)
    r1 = x - hi.astype(F32)
    mid = r1.astype(BF16)
    lo = (r1 - mid.astype(F32)).astype(BF16)
    dot = lambda t: jnp.dot(tri_b, t, preferred_element_type=F32)
    return dot(hi) + dot(mid) + dot(lo)


def _gla_kernel(*refs, reverse, is_first, final, step):
    if final:
        (q_ref, k_ref, v_ref, lr_ref, w2_ref, gb_ref, fwd_ref, gr_ref, ng_ref,
         o_ref, st_ref) = refs
    else:
        q_ref, k_ref, v_ref, lr_ref, w2_ref, gb_ref, o_ref, st_ref = refs
    c, n_rows = GLA_CHUNK, GLA_SUPER
    nb = n_rows // c

    @pl.when(is_first(step(*(pl.program_id(a) for a in range(3)))))
    def _():
        st_ref[...] = jnp.zeros_like(st_ref)

    gate = jnp.dot(lr_ref[...].astype(BF16), w2_ref[...], preferred_element_type=F32) + gb_ref[...]
    log_a_all = ((jnp.minimum(gate, 0.0) - jnp.log1p(jnp.exp(-jnp.abs(gate))))
                 * (np.log2(np.e) / GLA_GATE_NORM))

    row = lax.broadcasted_iota(jnp.int32, (n_rows, n_rows), 0)
    col = lax.broadcasted_iota(jnp.int32, (n_rows, n_rows), 1)
    before = (col > row) if reverse else (col < row)
    upto = (col >= row) if reverse else (col <= row)
    same_chunk = (row // c) == (col // c)
    same_pair = (row // (2 * c)) == (col // (2 * c))
    m_chunk = jnp.logical_and(same_chunk, upto)
    m_pair = jnp.logical_and(jnp.logical_and(same_pair, jnp.logical_not(same_chunk)), before)
    m_cross = jnp.logical_and(jnp.logical_not(same_pair), before)

    tri_b = upto.astype(BF16)
    order = list(reversed(range(nb))) if reverse else list(range(nb))
    per_chunk = lambda r: jnp.concatenate(
        [jnp.broadcast_to(r[b], (c, GLA_DK)) for b in range(nb)], axis=0)
    nt = lambda a, b: lax.dot_general(a, b, NT_DIMS, preferred_element_type=F32)

    yield

    for h in range(GLA_HEADS):
        dk = slice(h * GLA_DK, (h + 1) * GLA_DK)
        dv = slice(h * GLA_DV, (h + 1) * GLA_DV)
        cum = _cumsum_rows(tri_b, log_a_all[:, dk])
        r_start, r_end = {}, {}
        r_prev = jnp.zeros((1, GLA_DK), F32)
        for b in order:
            e = b * c if reverse else b * c + c - 1
            r_start[b] = r_prev
            r_end[b] = cum[e:e + 1]
            r_prev = r_end[b]
        total = r_prev
        mid = r_end[order[nb // 2 - 1]]
        cumloc = cum - per_chunk(r_start)
        to_end = per_chunk(r_end) - cum

        q = q_ref[:, dk] * (GLA_DK ** -0.5)
        k = k_ref[:, dk]
        v = v_ref[:, dv].astype(BF16)
        q_loc = (q * jnp.exp2(cumloc)).astype(BF16)
        k_loc = (k * jnp.exp2(-cumloc)).astype(BF16)
        k_end = (k * jnp.exp2(to_end)).astype(BF16)
        q_mid = (q * jnp.exp2(jnp.minimum(cum - mid, 0.0))).astype(BF16)
        k_mid = (k * jnp.exp2(jnp.minimum(mid - cum, 0.0))).astype(BF16)
        q_all = (q * jnp.exp2(cum)).astype(BF16)
        k_all = (k * jnp.exp2(total - cum)).astype(BF16)

        attn = jnp.where(m_chunk, nt(q_loc, k_loc),
                         jnp.where(m_pair, nt(q_loc, k_end),
                                   jnp.where(m_cross, nt(q_mid, k_mid), 0.0))).astype(BF16)
        state = st_ref[h]
        o = jnp.dot(attn, v, preferred_element_type=F32) + nt(q_all, state.astype(BF16))
        st_ref[h] = state * jnp.exp2(total) + lax.dot_general(
            v, k_all, TN_DIMS, preferred_element_type=F32)
        if final:
            tot = fwd_ref[:, dv] + o
            gr = gr_ref[:, dv]
            o_ref[:, dv] = (_rms_norm(tot, ng_ref[...]) * (gr * _sigmoid(gr))).astype(o_ref.dtype)
        else:
            o_ref[:, dv] = o
        yield


def _gla_job(p, p_glr, offs, w2cat, gbcat, groups, step, *, reverse, fwd=None, norm_g=None):
    t = p.shape[0]
    final = fwd is not None
    lb = GLA_SUPER
    (_, _, seq_a), (off_b, _, seq_b) = groups
    n_a, ns_a, ns_b = off_b // lb, seq_a // lb, seq_b // lb
    assert seq_a % lb == 0 and seq_b % lb == 0
    d = 1 if reverse else 0

    def local(u):
        in_a = u < n_a
        return jnp.where(in_a, u % ns_a, (u - n_a) % ns_b), jnp.where(in_a, ns_a, ns_b)

    def rows(u):
        n, ns = local(u)
        return (u - n) + (ns - 1 - n) if reverse else u

    is_first = lambda u: local(u)[0] == 0
    key, val = GLA_HEADS * GLA_DK, GLA_HEADS * GLA_DV
    assert offs["gq"] % key == 0 and offs["gk"] % key == 0
    assert offs["gv"] % val == 0 and offs["gr"] % val == 0
    col = lambda off, width: (lambda *ids: (rows(step(*ids)), off // width))
    in_specs = [pl.BlockSpec((lb, key), col(offs["gq"], key)),
                pl.BlockSpec((lb, key), col(offs["gk"], key)),
                pl.BlockSpec((lb, val), col(offs["gv"], val)),
                pl.BlockSpec((lb, LANES), col(0, LANES)),
                pl.BlockSpec((LANES, key), lambda *ids: (0, d)),
                pl.BlockSpec((1, key), lambda *ids: (0, d))]
    args = [p, p, p, p_glr, w2cat, gbcat]
    if final:
        in_specs += [pl.BlockSpec((lb, val), col(0, val)),
                     pl.BlockSpec((lb, val), col(offs["gr"], val)),
                     pl.BlockSpec((1, GLA_DV), lambda *ids: (0, 0))]
        args += [fwd, p, norm_g.reshape(1, GLA_DV)]
    return dict(
        kernel=functools.partial(_gla_kernel, reverse=reverse, is_first=is_first, final=final,
                                 step=step),
        in_specs=in_specs, args=args,
        out_spec=pl.BlockSpec((lb, val), col(0, val)),
        out_shape=jax.ShapeDtypeStruct((t, val), BF16 if final else F32),
        scratch_shapes=[pltpu.VMEM((GLA_HEADS, GLA_DV, GLA_DK), F32)])


def _merge_kernel(a1_ref, a2_ref, w1_ref, w2_ref, g1_ref, g2_ref, b1_ref, b2_ref, o_ref):
    y1 = jnp.dot(a1_ref[...], w1_ref[...], preferred_element_type=F32)
    y2 = jnp.dot(a2_ref[...], w2_ref[...], preferred_element_type=F32)
    s1 = _sigmoid(g1_ref[...] + b1_ref[...])
    s2 = _sigmoid(g2_ref[...] + b2_ref[...])
    o_ref[...] = (s1 * y1 + s2 * y2).astype(o_ref.dtype)


def _branch_merge(o_mla, o_gla, w_mla, w_gla, p, gm_off, b_merge):
    t, k1 = o_mla.shape
    k2 = o_gla.shape[1]
    d = w_mla.shape[1]
    bm, bn = _block(t, 1024), _block(np.gcd(d, gm_off), 1024)
    g0, nd = gm_off // bn, d // bn
    return pl.pallas_call(
        _merge_kernel,
        grid=(t // bm, nd),
        in_specs=[pl.BlockSpec((bm, k1), lambda i, j: (i, 0)),
                  pl.BlockSpec((bm, k2), lambda i, j: (i, 0)),
                  pl.BlockSpec((k1, bn), lambda i, j: (0, j)),
                  pl.BlockSpec((k2, bn), lambda i, j: (0, j)),
                  pl.BlockSpec((bm, bn), lambda i, j: (i, g0 + j)),
                  pl.BlockSpec((bm, bn), lambda i, j: (i, g0 + nd + j)),
                  pl.BlockSpec((1, bn), lambda i, j: (0, j)),
                  pl.BlockSpec((1, bn), lambda i, j: (0, nd + j))],
        out_specs=pl.BlockSpec((bm, bn), lambda i, j: (i, j)),
        out_shape=jax.ShapeDtypeStruct((t, d), BF16),
        compiler_params=_params(2, VMEM_LIMIT_LARGE_BYTES),
        name="branch_merge",
    )(o_mla, o_gla, w_mla, w_gla, p, p, b_merge.reshape(1, 2 * d), b_merge.reshape(1, 2 * d))


def _xattn_kernel(q_ref, k_ref, v_ref, o_ref, *, scale):
    hd = q_ref.shape[1] // XA_HEADS
    for h in range(XA_HEADS):
        cols = slice(h * hd, (h + 1) * hd)
        s = lax.dot_general(q_ref[:, cols], k_ref[:, cols], NT_DIMS,
                            preferred_element_type=F32) * scale
        o_ref[:, cols] = _softmax_pv(s, v_ref[:, cols]).astype(o_ref.dtype)


def _cross_attention(q, kv, groups, n_mem):
    t, d = q.shape
    (_, batch_a, seq_a), (off_b, _, seq_b) = groups
    bq = _block(np.gcd(seq_a, seq_b), 512)
    n_a, per_a, per_b = off_b // bq, seq_a // bq, seq_b // bq
    mem_row = lambda u: jnp.where(u < n_a, u // per_a, batch_a + (u - n_a) // per_b)
    return pl.pallas_call(
        functools.partial(_xattn_kernel, scale=(d // XA_HEADS) ** -0.5),
        grid=(t // bq,),
        in_specs=[pl.BlockSpec((bq, d), lambda u: (u, 0)),
                  pl.BlockSpec((n_mem, d), lambda u: (mem_row(u), 0)),
                  pl.BlockSpec((n_mem, d), lambda u: (mem_row(u), 1))],
        out_specs=pl.BlockSpec((bq, d), lambda u: (u, 0)),
        out_shape=jax.ShapeDtypeStruct((t, d), BF16),
        compiler_params=_params(1),
        name="cross_attention",
    )(q, kv, kv)


def _in_proj_layout(d_model, q_lora, kv_lora):
    gla_key, gla_val = GLA_HEADS * GLA_DK, GLA_HEADS * GLA_DV
    b_width = 2 * gla_key + 2 * gla_val
    s_kr = q_lora + kv_lora
    s_gq = s_kr + MLA_ROPE
    s_glr = s_gq + b_width
    s_gm = s_glr + 2 * GLA_GATE_RANK
    pb = 1024 if b_width % 1024 == 0 and (2 * d_model) % 1024 == 0 else 512
    assert b_width % pb == 0 and (2 * d_model) % pb == 0
    assert s_kr % LANES == 0 and q_lora % kv_lora == 0 and 2 * GLA_GATE_RANK <= LANES
    assert s_gq % 8 == 0 and s_glr % 8 == 0 and s_gm % 8 == 0
    n_a = -(-s_gq // pb)
    gq = n_a * pb
    starts = ([pb * j for j in range(n_a)] + [s_gq + pb * j for j in range(b_width // pb)]
              + [s_gm + pb * j for j in range(2 * d_model // pb)])
    offs = {"cq": 0, "ckv": q_lora, "krope": s_kr, "gq": gq, "gk": gq + gla_key,
            "gv": gq + 2 * gla_key, "gr": gq + 2 * gla_key + gla_val, "gm": gq + b_width}
    return offs, starts, pb, s_glr


BF16_SUBLANES = 16


def _in_proj_kernel(*refs, n_cast, cast_steps, n_cols):
    a_ref, wt_ref, wg_ref = refs[:3]
    cast_in = refs[3:3 + n_cast]
    o_ref, og_ref = refs[3 + n_cast:5 + n_cast]
    cast_out = refs[5 + n_cast:]
    o_ref[...] = lax.dot_general(a_ref[...], wt_ref[...], NT_DIMS, preferred_element_type=F32)

    @pl.when(pl.program_id(1) == 0)
    def _():
        og_ref[...] = lax.dot_general(a_ref[...], wg_ref[...], NT_DIMS,
                                      preferred_element_type=F32)

    step = pl.program_id(0) * n_cols + pl.program_id(1)
    for src, dst, active in zip(cast_in, cast_out, cast_steps):
        @pl.when(step < active)
        def _(src=src, dst=dst):
            dst[...] = src[...].astype(BF16)


def _in_proj(a, w_t, starts, bn, gate_start, *, bm, name, cast_weights):
    m, kdim = a.shape
    bm = _block(m, bm)
    n_cols = len(starts)
    n_steps = (m // bm) * n_cols

    assert all(s % BF16_SUBLANES == 0 for s in starts) and gate_start % BF16_SUBLANES == 0

    def row_start(j):
        r = jnp.int32(starts[-1] // BF16_SUBLANES)
        for idx in range(n_cols - 2, -1, -1):
            r = jnp.where(j == idx, jnp.int32(starts[idx] // BF16_SUBLANES), r)
        return pl.multiple_of(r * BF16_SUBLANES, BF16_SUBLANES)

    cast_specs, cast_steps = [], []
    for w in cast_weights:
        rows = BF16_SUBLANES
        while w.shape[0] % rows or w.shape[0] // rows > n_steps:
            rows += BF16_SUBLANES
        active = w.shape[0] // rows
        cast_steps.append(active)
        cast_specs.append(pl.BlockSpec(
            (rows, w.shape[1]),
            lambda i, j, active=active: (jnp.minimum(i * n_cols + j, active - 1), 0)))

    out = pl.pallas_call(
        functools.partial(_in_proj_kernel, n_cast=len(cast_weights),
                          cast_steps=tuple(cast_steps), n_cols=n_cols),
        grid=(m // bm, n_cols),
        in_specs=[pl.BlockSpec((bm, kdim), lambda i, j: (i, 0)),
                  pl.BlockSpec((pl.Element(bn), pl.Element(kdim)),
                               lambda i, j: (row_start(j), 0)),
                  pl.BlockSpec((pl.Element(LANES), pl.Element(kdim)),
                               lambda i, j: (gate_start, 0))] + cast_specs,
        out_specs=[pl.BlockSpec((bm, bn), lambda i, j: (i, j)),
                   pl.BlockSpec((bm, LANES), lambda i, j: (i, 0))] + cast_specs,
        out_shape=[jax.ShapeDtypeStruct((m, bn * n_cols), F32),
                   jax.ShapeDtypeStruct((m, LANES), F32)]
        + [jax.ShapeDtypeStruct(w.shape, BF16) for w in cast_weights],
        compiler_params=_params(2, VMEM_LIMIT_LARGE_BYTES),
        name=name,
    )(a, w_t, w_t, *cast_weights)
    return out[0], out[1], tuple(out[2:])


def _pack_uq(w_uq):
    q_lora = w_uq.shape[0]
    w = w_uq.reshape(q_lora, MLA_HEADS, MLA_NOPE + MLA_ROPE)
    rope = w[:, :, MLA_NOPE:]
    half = MLA_ROPE // 2
    rot = jnp.concatenate([-rope[:, :, half:], rope[:, :, :half]], axis=2)
    return jnp.concatenate([w, rot], axis=2).reshape(q_lora, MLA_HEADS * MLA_QK_PAD).astype(BF16)


def _pack_ukv(w_ukv):
    kv_lora = w_ukv.shape[0]
    w = w_ukv.reshape(kv_lora, MLA_HEADS, MLA_NOPE + MLA_V)
    wk = w[:, :, :MLA_NOPE].reshape(kv_lora, MLA_HEADS * MLA_NOPE)
    wv = w[:, :, MLA_NOPE:].reshape(kv_lora, MLA_HEADS * MLA_V)
    return wk.astype(BF16), wv.astype(BF16)


def _pack_gate(w2, gb):
    r, key = w2.shape[1], w2.shape[2]
    z = jnp.zeros((r, key), w2.dtype)
    top = jnp.concatenate([w2[0], z], axis=1)
    bot = jnp.concatenate([z, w2[1]], axis=1)
    tail = jnp.zeros((LANES - 2 * r, 2 * key), w2.dtype)
    return jnp.concatenate([top, bot, tail], axis=0).astype(BF16), gb.reshape(1, 2 * key)


def kernel(x_prompt, x_sample, mem_prompt, mem_sample, ln_in_g, ln_in_b, w_in, b_merge, mla_q_norm, w_uq, mla_kv_norm, w_ukv, gla_gate_w2, gla_gate_b, gla_norm, w_branch_mla, w_branch_gla, w_mix_out, ln1_g, ln1_b, xa_wq, xa_wkv, xa_wo, ln2_g, ln2_b, mlp_w1, mlp_w2, ln3_g, ln3_b):
    assert w_in.shape[0] == DEPTH
    ba, sa, d = x_prompt.shape
    bb, sb, _ = x_sample.shape
    n_mem = mem_prompt.shape[1]
    ta, tb = ba * sa, bb * sb
    groups = ((0, ba, sa), (ta, bb, sb))
    q_lora, kv_lora = mla_q_norm.shape[1], mla_kv_norm.shape[1]
    assert ta % sb == 0 and ta % 1024 == 0 and sa % 1024 == 0 and sb % 1024 == 0

    offs, in_starts, in_bn, glr_start = _in_proj_layout(d, q_lora, kv_lora)
    w_in_t = jnp.swapaxes(w_in[0], 0, 1).astype(BF16)
    wq_p = _pack_uq(w_uq[0])
    wk_p, wv_p = _pack_ukv(w_ukv[0])
    w2cat, gbcat = _pack_gate(gla_gate_w2[0], gla_gate_b[0])
    cos, sin = _rope_tables(max(sa, sb))

    h, h_b = _ln_in(x_prompt.reshape(ta, d), x_sample.reshape(tb, d), ln_in_g, ln_in_b)

    later_weights = (w_branch_mla[0], w_branch_gla[0], w_mix_out[0], xa_wq[0], xa_wkv[0],
                     xa_wo[0], mlp_w1[0], mlp_w2[0])
    (p, p_glr,
     (w_br_mla_b, w_br_gla_b, w_mix_b, xa_wq_b, xa_wkv_b, xa_wo_b, mlp_w1_b, mlp_w2_b)) = _in_proj(
        h_b, w_in_t, in_starts, in_bn, glr_start, bm=1024, name="in_proj",
        cast_weights=later_weights)
    q = _q_proj(p, offs["cq"], q_lora, mla_q_norm[0], wq_p, cos, sin, groups, 1024)
    k, v = _kv_proj(p, offs["ckv"], kv_lora, offs["krope"], mla_kv_norm[0], wk_p, wv_p,
                    cos, sin, groups, 512)
    def gla_pass(g, step, fwd):
        return _gla_job(p, p_glr, offs, w2cat, gbcat, groups, step, reverse=g == 1,
                        fwd=fwd, norm_g=gla_norm[0])

    o_mla, o_gla = _mixer_attention(q, k, v, groups, gla_pass)
    merged = _branch_merge(o_mla, o_gla, w_br_mla_b, w_br_gla_b, p, offs["gm"], b_merge[0])
    h, h_b = _matmul_ln(merged, w_mix_b, h, ln1_g[0], ln1_b[0], bm=256, name="mix_out_ln")

    mem = jnp.concatenate([mem_prompt.reshape(ba * n_mem, d), mem_sample.reshape(bb * n_mem, d)])
    xq = _matmul_piped(h_b, xa_wq_b, BF16, name="xa_q")
    xkv = _matmul(mem.astype(BF16), xa_wkv_b, BF16, bm=2048, name="xa_kv")
    o_x = _cross_attention(xq, xkv, groups, n_mem)
    h, h_b = _matmul_ln(o_x, xa_wo_b, h, ln2_g[0], ln2_b[0], bm=256, name="xa_o_ln")

    u = _matmul_piped(h_b, mlp_w1_b, BF16, relu2=True, name="mlp_up")
    z = _matmul_res(u, mlp_w2_b, h, name="mlp_down")
    y_a = _ln_rows(z, ln3_g[0], ln3_b[0], row_off=0, rows=ta)
    y_b = _ln_rows(z, ln3_g[0], ln3_b[0], row_off=ta, rows=tb)
    return y_a.reshape(ba, sa, d), y_b.reshape(bb, sb, d)
```
